```python
import math
import jax, jax.numpy as jnp
from jax import lax
import numpy as np

D_MODEL = 2048
BATCH = 4
SEQ = 2048
DEPTH = 4
DEC_BATCH = 128
DEC_SEQ = 8
PAST_LEN = 8192
PAGE_SIZE = 128

MIX_WIDTH = D_MODEL
SSM_WIDTH = MIX_WIDTH // 2
SSM_GROUP = 16
SSM_GROUPS = SSM_WIDTH // SSM_GROUP
SSM_STATE = 64
MLA_HEADS = 8
MLA_NOPE = 128
MLA_ROPE = 64
MLA_V = 128
MLA_WIDTH = MLA_HEADS * MLA_V
Q_LORA = D_MODEL // 4
KV_LORA = D_MODEL // 8
ROPE_BASE = 10000.0
IN_COLS = SSM_WIDTH + Q_LORA + KV_LORA + MLA_ROPE
MEM_LEN = 256
MEM_HEADS = 4
MEM_HEAD_DIM = 128
D_FF = 11 * D_MODEL // 4
N_EXPERTS = 8
TOP_K = 2
EXPERT_FF = D_FF // 2
N_DENSE = (DEPTH + 1) // 2
N_MOE = DEPTH // 2
Q_BLOCK = 128
DN_ALPHA = (2.0 * DEPTH) ** 0.25
DN_BETA = (8.0 * DEPTH) ** -0.25
LN_EPS = 1e-5
RMS_EPS = 1e-6

kernel_name = 'hymba_s5_mla_deepnorm_decoder'


def _layernorm(x, g, b):
    xf = x.astype(jnp.float32)
    xc = xf - jnp.mean(xf, -1, keepdims=True)
    var = jnp.mean(xc * xc, -1, keepdims=True)
    return (xc * lax.rsqrt(var + LN_EPS) * g.astype(jnp.float32) + b.astype(jnp.float32)).astype(x.dtype)


def _rmsnorm(x, g):
    xf = x.astype(jnp.float32)
    return (xf * lax.rsqrt(jnp.mean(xf * xf, -1, keepdims=True) + RMS_EPS) * g.astype(jnp.float32)).astype(x.dtype)


def _rope(x, pos):
    half = MLA_ROPE // 2
    inv = ROPE_BASE ** (-jnp.arange(half, dtype=jnp.float32) / half)
    ang = pos.astype(jnp.float32)[:, None] * inv[None, :]
    cos = jnp.cos(ang)[:, None, :]
    sin = jnp.sin(ang)[:, None, :]
    xf = x.astype(jnp.float32)
    x1, x2 = xf[..., :half], xf[..., half:]
    return jnp.concatenate([x1 * cos - x2 * sin, x1 * sin + x2 * cos], -1).astype(x.dtype)


def _ssm_combine(e1, e2):
    a1r, a1i, b1r, b1i = e1
    a2r, a2i, b2r, b2i = e2
    return (a2r * a1r - a2i * a1i,
            a2r * a1i + a2i * a1r,
            a2r * b1r - a2i * b1i + b2r,
            a2r * b1i + a2i * b1r + b2i)


def _s5_branch(u, s0_re, s0_im, lam_re, lam_im, log_dt, b_re, b_im, c_re, c_im, d, w_glu):
    bt, t = u.shape[:2]
    uf = u.astype(jnp.float32)
    ug = uf.reshape(bt, t, SSM_GROUPS, SSM_GROUP)
    lr = lam_re.astype(jnp.float32)
    li = lam_im.astype(jnp.float32)
    dt = jnp.exp(log_dt.astype(jnp.float32))[:, None]
    mag = jnp.exp(lr * dt)
    ar = mag * jnp.cos(li * dt)
    ai = mag * jnp.sin(li * dt)
    den = lr * lr + li * li
    zr = ((ar - 1.0) * lr + ai * li) / den
    zi = (ai * lr - (ar - 1.0) * li) / den
    br = b_re.astype(jnp.float32)
    bi = b_im.astype(jnp.float32)
    bbr = zr[..., None] * br - zi[..., None] * bi
    bbi = zr[..., None] * bi + zi[..., None] * br
    xr = jnp.einsum('gpc,btgc->btgp', bbr, ug)
    xi = jnp.einsum('gpc,btgc->btgp', bbi, ug)
    xr = xr.at[:, 0].add(ar * s0_re - ai * s0_im)
    xi = xi.at[:, 0].add(ar * s0_im + ai * s0_re)
    a_r = jnp.broadcast_to(ar, xr.shape)
    a_i = jnp.broadcast_to(ai, xr.shape)
    _, _, sr, si = lax.associative_scan(_ssm_combine, (a_r, a_i, xr, xi), axis=1)
    y = (jnp.einsum('gcp,btgp->btgc', c_re.astype(jnp.float32), sr)
         - jnp.einsum('gcp,btgp->btgc', c_im.astype(jnp.float32), si))
    y = y.reshape(bt, t, SSM_WIDTH) + d.astype(jnp.float32) * uf
    g = jax.nn.gelu(y)
    out = g * jax.nn.sigmoid(g @ w_glu.astype(jnp.float32))
    return out.astype(u.dtype), sr[:, -1], si[:, -1]


def _mla_core(q_lat, q_rope, ckv, krope, q_pos, k_pos):
    scale = (MLA_NOPE + MLA_ROPE) ** -0.5
    s = (jnp.einsum('bthr,bsr->bhts', q_lat, ckv, preferred_element_type=jnp.float32)
         + jnp.einsum('bthe,bse->bhts', q_rope, krope, preferred_element_type=jnp.float32)) * scale
    causal = (k_pos[None, :] <= q_pos[:, None])[None, None]
    s = jnp.where(causal, s, jnp.finfo(jnp.float32).min)
    p = jax.nn.softmax(s, axis=-1)
    return jnp.einsum('bhts,bsr->bthr', p.astype(ckv.dtype), ckv)


def _prompt_attend(q_lat, q_rope, ckv, krope):
    bt, t = q_lat.shape[:2]
    nb = t // Q_BLOCK
    pos = jnp.arange(t)

    def blk(args):
        ql, qr, qp = args
        return _mla_core(ql, qr, ckv, krope, qp, pos)

    qlb = q_lat.reshape(bt, nb, Q_BLOCK, MLA_HEADS, KV_LORA).swapaxes(0, 1)
    qrb = q_rope.reshape(bt, nb, Q_BLOCK, MLA_HEADS, MLA_ROPE).swapaxes(0, 1)
    out = lax.map(blk, (qlb, qrb, pos.reshape(nb, Q_BLOCK)))
    return out.swapaxes(0, 1).reshape(bt, t, MLA_HEADS, KV_LORA)


def _paged_attend(ckv_pages, krope_pages, page_table):
    def attend(q_lat, q_rope, ckv, krope):
        bt, t = q_lat.shape[:2]
        past = page_table.shape[1] * ckv_pages.shape[1]
        past_c = ckv_pages[page_table].reshape(bt, past, KV_LORA)
        past_r = krope_pages[page_table].reshape(bt, past, MLA_ROPE)
        keys_c = jnp.concatenate([past_c, ckv.astype(past_c.dtype)], axis=1)
        keys_r = jnp.concatenate([past_r, krope.astype(past_r.dtype)], axis=1)
        k_pos = jnp.arange(past + t)
        q_pos = past + jnp.arange(t)
        return _mla_core(q_lat, q_rope, keys_c, keys_r, q_pos, k_pos)
    return attend


def _mixer(x, pos, s0_re, s0_im, attend, w_in, g_q, w_uq, g_kv, w_uk, w_uv,
           lam_re, lam_im, log_dt, b_re, b_im, c_re, c_im, d, w_glu, g_ssm_out, g_mla_out, w_out):
    bt, t = x.shape[:2]
    h = x @ w_in
    o1 = SSM_WIDTH
    o2 = o1 + Q_LORA
    o3 = o2 + KV_LORA
    u, c_q, c_kv, k_r = h[..., :o1], h[..., o1:o2], h[..., o2:o3], h[..., o3:]
    ssm_out, s_re, s_im = _s5_branch(u, s0_re, s0_im, lam_re, lam_im, log_dt,
                                     b_re, b_im, c_re, c_im, d, w_glu)
    q = jnp.einsum('btr,rhe->bthe', _rmsnorm(c_q, g_q), w_uq)
    q_nope = q[..., :MLA_NOPE]
    q_rope = _rope(q[..., MLA_NOPE:], pos)
    ckv = _rmsnorm(c_kv, g_kv)
    krope = _rope(k_r[:, :, None, :], pos)[:, :, 0, :]
    q_lat = jnp.einsum('bthn,rhn->bthr', q_nope, w_uk)
    o_lat = attend(q_lat, q_rope, ckv, krope)
    mla_out = jnp.einsum('bthr,rhv->bthv', o_lat, w_uv).reshape(bt, t, MLA_WIDTH)
    merged = jnp.concatenate([_rmsnorm(ssm_out, g_ssm_out), _rmsnorm(mla_out, g_mla_out)], -1)
    return merged @ w_out, ckv, krope, s_re, s_im


def _mem_kv(mem, w_mk, w_mv):
    return (jnp.einsum('bmd,dhk->bmhk', mem, w_mk), jnp.einsum('bmd,dhk->bmhk', mem, w_mv))


def _cross(x, mem_k, mem_v, w_mq, w_mo):
    q = jnp.einsum('btd,dhk->bthk', x, w_mq)
    s = jnp.einsum('bthk,bmhk->bhtm', q, mem_k, preferred_element_type=jnp.float32) * MEM_HEAD_DIM ** -0.5
    p = jax.nn.softmax(s, axis=-1)
    o = jnp.einsum('bhtm,bmhk->bthk', p.astype(mem_v.dtype), mem_v)
    return jnp.einsum('bthk,hkd->btd', o, w_mo)


def _swiglu(x, wg, wu, wd):
    return (jax.nn.silu(x @ wg) * (x @ wu)) @ wd


def _moe(x, w_router, wg, wu, wd):
    bt, t, dm = x.shape
    xf = x.reshape(bt * t, dm)
    probs = jax.nn.softmax((xf @ w_router).astype(jnp.float32), axis=-1)
    top_v, top_i = lax.top_k(probs, TOP_K)
    top_v = top_v / jnp.sum(top_v, -1, keepdims=True)
    gates = jnp.sum(jax.nn.one_hot(top_i, N_EXPERTS, dtype=jnp.float32) * top_v[..., None], axis=1)
    y = jnp.zeros(xf.shape, jnp.float32)
    for e in range(N_EXPERTS):
        y = y + gates[:, e:e + 1] * _swiglu(xf, wg[e], wu[e], wd[e]).astype(jnp.float32)
    return y.astype(x.dtype).reshape(bt, t, dm)


def _channel_mixer(l, x, w_ff_gate, w_ff_up, w_ff_down, w_router, w_e_gate, w_e_up, w_e_down):
    i = l // 2
    if l % 2 == 0:
        return _swiglu(x, w_ff_gate[i], w_ff_up[i], w_ff_down[i])
    return _moe(x, w_router[i], w_e_gate[i], w_e_up[i], w_e_down[i])


def setup_inputs(seed: int = 0) -> dict:
    key = jax.random.key(seed)
    ks = iter(jax.random.split(key, 64))
    f32 = jnp.float32

    def nrm(shape, scale):
        return jax.random.normal(next(ks), shape, f32) * scale

    def gain(shape):
        return 1.0 + nrm(shape, 0.01)

    n_pages = PAST_LEN // PAGE_SIZE
    n_pool = (DEC_BATCH * n_pages * 5) // 4
    G, P = SSM_GROUPS, SSM_STATE
    return {
        'x_prompt': nrm((BATCH, SEQ, D_MODEL), 1.0),
        'x_sample': nrm((DEC_BATCH, DEC_SEQ, D_MODEL), 1.0),
        'mem_prompt': nrm((BATCH, MEM_LEN, D_MODEL), 1.0),
        'cache_ckv': nrm((DEPTH, n_pool, PAGE_SIZE, KV_LORA), 1.0),
        'cache_krope': nrm((DEPTH, n_pool, PAGE_SIZE, MLA_ROPE), 1.0),
        'cache_mem_k': nrm((DEPTH, DEC_BATCH, MEM_LEN, MEM_HEADS, MEM_HEAD_DIM), 1.0),
        'cache_mem_v': nrm((DEPTH, DEC_BATCH, MEM_LEN, MEM_HEADS, MEM_HEAD_DIM), 1.0),
        'state_ssm_re': nrm((DEPTH, DEC_BATCH, G, P), 0.1),
        'state_ssm_im': nrm((DEPTH, DEC_BATCH, G, P), 0.1),
        'page_table': jax.random.permutation(next(ks), n_pool)[:DEC_BATCH * n_pages]
                      .reshape(DEC_BATCH, n_pages).astype(jnp.int32),
        'w_in': nrm((DEPTH, D_MODEL, IN_COLS), D_MODEL ** -0.5),
        'g_q': gain((DEPTH, Q_LORA)),
        'w_uq': nrm((DEPTH, Q_LORA, MLA_HEADS, MLA_NOPE + MLA_ROPE), Q_LORA ** -0.5),
        'g_kv': gain((DEPTH, KV_LORA)),
        'w_uk': nrm((DEPTH, KV_LORA, MLA_HEADS, MLA_NOPE), KV_LORA ** -0.5),
        'w_uv': nrm((DEPTH, KV_LORA, MLA_HEADS, MLA_V), KV_LORA ** -0.5),
        'ssm_lambda_re': -0.5 + nrm((DEPTH, G, P), 1e-3),
        'ssm_lambda_im': math.pi * jnp.arange(P, dtype=f32) + nrm((DEPTH, G, P), 1e-3),
        'ssm_log_dt': jax.random.uniform(next(ks), (DEPTH, G), f32,
                                         minval=math.log(1e-3), maxval=math.log(1e-1)),
        'ssm_b_re': nrm((DEPTH, G, P, SSM_GROUP), (2.0 * SSM_GROUP) ** -0.5),
        'ssm_b_im': nrm((DEPTH, G, P, SSM_GROUP), (2.0 * SSM_GROUP) ** -0.5),
        'ssm_c_re': nrm((DEPTH, G, SSM_GROUP, P), (2.0 * P) ** -0.5),
        'ssm_c_im': nrm((DEPTH, G, SSM_GROUP, P), (2.0 * P) ** -0.5),
        'ssm_d': nrm((DEPTH, SSM_WIDTH), 1.0),
        'w_glu': nrm((DEPTH, SSM_WIDTH, SSM_WIDTH), SSM_WIDTH ** -0.5),
        'g_ssm_out': gain((DEPTH, SSM_WIDTH)),
        'g_mla_out': gain((DEPTH, MLA_WIDTH)),
        'w_out': nrm((DEPTH, MIX_WIDTH, D_MODEL), MIX_WIDTH ** -0.5 * DN_BETA),
        'ln1_g': gain((DEPTH, D_MODEL)),
        'ln1_b': nrm((DEPTH, D_MODEL), 0.01),
        'w_mq': nrm((DEPTH, D_MODEL, MEM_HEADS, MEM_HEAD_DIM), D_MODEL ** -0.5),
        'w_mk': nrm((DEPTH, D_MODEL, MEM_HEADS, MEM_HEAD_DIM), D_MODEL ** -0.5),
        'w_mv': nrm((DEPTH, D_MODEL, MEM_HEADS, MEM_HEAD_DIM), D_MODEL ** -0.5),
        'w_mo': nrm((DEPTH, MEM_HEADS, MEM_HEAD_DIM, D_MODEL), (MEM_HEADS * MEM_HEAD_DIM) ** -0.5 * DN_BETA),
        'ln2_g': gain((DEPTH, D_MODEL)),
        'ln2_b': nrm((DEPTH, D_MODEL), 0.01),
        'w_ff_gate': nrm((N_DENSE, D_MODEL, D_FF), D_MODEL ** -0.5),
        'w_ff_up': nrm((N_DENSE, D_MODEL, D_FF), D_MODEL ** -0.5),
        'w_ff_down': nrm((N_DENSE, D_FF, D_MODEL), D_FF ** -0.5 * DN_BETA),
        'w_router': nrm((N_MOE, D_MODEL, N_EXPERTS), D_MODEL ** -0.5),
        'w_e_gate': nrm((N_MOE, N_EXPERTS, D_MODEL, EXPERT_FF), D_MODEL ** -0.5),
        'w_e_up': nrm((N_MOE, N_EXPERTS, D_MODEL, EXPERT_FF), D_MODEL ** -0.5),
        'w_e_down': nrm((N_MOE, N_EXPERTS, EXPERT_FF, D_MODEL), EXPERT_FF ** -0.5 * DN_BETA),
        'ln3_g': gain((DEPTH, D_MODEL)),
        'ln3_b': nrm((DEPTH, D_MODEL), 0.01),
    }


def reference(x_prompt, x_sample, mem_prompt, cache_ckv, cache_krope, cache_mem_k, cache_mem_v,
              state_ssm_re, state_ssm_im, page_table, w_in, g_q, w_uq, g_kv, w_uk, w_uv,
              ssm_lambda_re, ssm_lambda_im, ssm_log_dt, ssm_b_re, ssm_b_im, ssm_c_re, ssm_c_im,
              ssm_d, w_glu, g_ssm_out, g_mla_out, w_out, ln1_g, ln1_b, w_mq, w_mk, w_mv, w_mo,
              ln2_g, ln2_b, w_ff_gate, w_ff_up, w_ff_down, w_router, w_e_gate, w_e_up, w_e_down,
              ln3_g, ln3_b):
    bp, tp = x_prompt.shape[:2]
    ts = x_sample.shape[1]
    past = page_table.shape[1] * cache_ckv.shape[2]
    pos_p = jnp.arange(tp)
    pos_s = past + jnp.arange(ts)
    zero_state = jnp.zeros((bp, SSM_GROUPS, SSM_STATE), jnp.float32)
    ffn_w = (w_ff_gate, w_ff_up, w_ff_down, w_router, w_e_gate, w_e_up, w_e_down)

    xp, xs = x_prompt, x_sample
    p_ckv, p_kr, p_sr, p_si, p_mk, p_mv = [], [], [], [], [], []
    s_ckv, s_kr, s_sr, s_si = [], [], [], []
    for l in range(DEPTH):
        mix_w = (w_in[l], g_q[l], w_uq[l], g_kv[l], w_uk[l], w_uv[l],
                 ssm_lambda_re[l], ssm_lambda_im[l], ssm_log_dt[l], ssm_b_re[l], ssm_b_im[l],
                 ssm_c_re[l], ssm_c_im[l], ssm_d[l], w_glu[l], g_ssm_out[l], g_mla_out[l], w_out[l])

        m, ckv, kr, sr, si = _mixer(xp, pos_p, zero_state, zero_state, _prompt_attend, *mix_w)
        xp = _layernorm(DN_ALPHA * xp + m, ln1_g[l], ln1_b[l])
        mk, mv = _mem_kv(mem_prompt, w_mk[l], w_mv[l])
        xp = _layernorm(DN_ALPHA * xp + _cross(xp, mk, mv, w_mq[l], w_mo[l]), ln2_g[l], ln2_b[l])
        xp = _layernorm(DN_ALPHA * xp + _channel_mixer(l, xp, *ffn_w), ln3_g[l], ln3_b[l])
        p_ckv.append(ckv)
        p_kr.append(kr)
        p_sr.append(sr)
        p_si.append(si)
        p_mk.append(mk)
        p_mv.append(mv)

        attend_s = _paged_attend(cache_ckv[l], cache_krope[l], page_table)
        m, ckv, kr, sr, si = _mixer(xs, pos_s, state_ssm_re[l].astype(jnp.float32),
                                    state_ssm_im[l].astype(jnp.float32), attend_s, *mix_w)
        xs = _layernorm(DN_ALPHA * xs + m, ln1_g[l], ln1_b[l])
        xs = _layernorm(DN_ALPHA * xs + _cross(xs, cache_mem_k[l], cache_mem_v[l], w_mq[l], w_mo[l]),
                        ln2_g[l], ln2_b[l])
        xs = _layernorm(DN_ALPHA * xs + _channel_mixer(l, xs, *ffn_w), ln3_g[l], ln3_b[l])
        s_ckv.append(ckv)
        s_kr.append(kr)
        s_sr.append(sr)
        s_si.append(si)

    y_prompt, y_sample = xp, xs
    new_p_ckv, new_p_kr = jnp.stack(p_ckv), jnp.stack(p_kr)
    new_p_sr, new_p_si = jnp.stack(p_sr), jnp.stack(p_si)
    new_p_mk, new_p_mv = jnp.stack(p_mk), jnp.stack(p_mv)
    new_s_ckv, new_s_kr = jnp.stack(s_ckv), jnp.stack(s_kr)
    new_s_sr, new_s_si = jnp.stack(s_sr), jnp.stack(s_si)
    return (y_prompt, y_sample, new_p_ckv, new_p_kr, new_p_sr, new_p_si, new_p_mk, new_p_mv,
            new_s_ckv, new_s_kr, new_s_sr, new_s_si)
```

```python
import functools
import math

import jax
import jax.numpy as jnp
from jax import lax
from jax.experimental import pallas as pl
from jax.experimental.pallas import tpu as pltpu

F32 = jnp.float32
BF16 = jnp.bfloat16

LANE = 128
SUBLANE = 8
VMEM_LIMIT = 56 * 2 ** 20
LN_EPS = 1e-5
RMS_EPS = 1e-6
ROPE_BASE = 10000.0
SLAB_GROUPS = 8
NEG = float(jnp.finfo(jnp.float32).min)


def _dot(a, b):
    return jnp.dot(a, b, preferred_element_type=F32)


def _dot_nt(a, b):
    return lax.dot_general(a, b, (((1,), (1,)), ((), ())), preferred_element_type=F32)


def _ln(v, g, b):
    vc = v - jnp.mean(v, -1, keepdims=True)
    var = jnp.mean(vc * vc, -1, keepdims=True)
    return vc * lax.rsqrt(var + LN_EPS) * g + b


def _rms(v, g):
    return v * lax.rsqrt(jnp.mean(v * v, -1, keepdims=True) + RMS_EPS) * g


def _tile(n, pref, mult=SUBLANE):
    best = None
    for t in range(mult, min(n, pref) + 1, mult):
        if n % t == 0:
            best = t
    assert best is not None, (n, pref, mult)
    return best


def _params(*sem):
    return pltpu.CompilerParams(dimension_semantics=sem, vmem_limit_bytes=VMEM_LIMIT)


def _rows(tm, w):
    return pl.BlockSpec((tm, w), lambda i: (i, 0))


def _const(shape):
    return pl.BlockSpec(shape, lambda i: (0,) * len(shape))


def _layer(l, shape):
    return pl.BlockSpec((None,) + shape, lambda i: (l,) + (0,) * len(shape))


def _rope_table_body(pos_ref, inv_ref, cs_ref):
    ang = pos_ref[...] * inv_ref[...]
    lane = lax.broadcasted_iota(jnp.int32, ang.shape, 1)
    c = jnp.cos(ang)
    s = jnp.sin(ang)
    cs_ref[...] = jnp.where(lane < LANE // 2, c, jnp.where(lane < 3 * LANE // 4, -s, s))


def _rope_table(pos, rope):
    n = pos.shape[0]
    half = rope // 2
    inv = ROPE_BASE ** (-jnp.arange(half, dtype=F32) / half)
    inv4 = jnp.tile(inv, 4)[None, :]
    tm = _tile(n, 1024)
    return pl.pallas_call(
        _rope_table_body, grid=(n // tm,),
        in_specs=[_rows(tm, 1), _const((1, LANE))], out_specs=_rows(tm, LANE),
        out_shape=jax.ShapeDtypeStruct((n, LANE), F32), compiler_params=_params("parallel"),
        name="rope_table")(pos, inv4)


def _inproj_body(x_ref, w_ref, gq_ref, gkv_ref, cs_ref, u_ref, cq_ref, kcat_ref, ckv_ref, kr_ref,
                 *, ssm_w, ql, kvl, rope):
    xb = x_ref[...].astype(BF16)
    o1, o2, o3 = ssm_w, ssm_w + ql, ssm_w + ql + kvl
    u_ref[...] = _dot(xb, w_ref[:, :o1])
    cq_ref[...] = _rms(_dot(xb, w_ref[:, o1:o2]), gq_ref[...]).astype(BF16)
    ckv = _rms(_dot(xb, w_ref[:, o2:o3]), gkv_ref[...])
    ckv_ref[...] = ckv
    kr2 = _dot(xb, w_ref[:, o3:o3 + LANE]) * cs_ref[...]
    kr = kr2 + pltpu.roll(kr2, LANE // 2, axis=1)
    kr_ref[...] = kr[:, :rope]
    kcat_ref[:, :kvl] = ckv.astype(BF16)
    kcat_ref[:, kvl:] = kr.astype(BF16)


def _inproj(x, w_ext, l, gq, gkv, cs, *, ssm_w, ql, kvl, rope):
    n, d = x.shape
    cols = w_ext.shape[2]
    tm = _tile(n, 512)
    body = functools.partial(_inproj_body, ssm_w=ssm_w, ql=ql, kvl=kvl, rope=rope)
    return pl.pallas_call(
        body, grid=(n // tm,),
        in_specs=[_rows(tm, d), _layer(l, (d, cols)), _layer(l, (1, ql)), _layer(l, (1, kvl)),
                  _rows(tm, LANE)],
        out_specs=[_rows(tm, ssm_w), _rows(tm, ql), _rows(tm, kvl + LANE), _rows(tm, kvl),
                   _rows(tm, rope)],
        out_shape=[jax.ShapeDtypeStruct((n, ssm_w), F32), jax.ShapeDtypeStruct((n, ql), BF16),
                   jax.ShapeDtypeStruct((n, kvl + LANE), BF16), jax.ShapeDtypeStruct((n, kvl), F32),
                   jax.ShapeDtypeStruct((n, rope), F32)],
        compiler_params=_params("parallel"), name="in_proj")(x, w_ext, gq, gkv, cs)


def _qproj_body(cq_ref, wuq_ref, wukt_ref, cs_ref, q_ref, *, heads, nope, kvl, scale):
    q = _dot(cq_ref[...], wuq_ref[...])
    cs = cs_ref[...]
    lane = lax.broadcasted_iota(jnp.int32, cs.shape, 1)
    hw = nope + LANE
    qw = kvl + LANE
    for h in range(heads):
        qlat = _dot(q[:, h * hw:h * hw + nope].astype(BF16), wukt_ref[h])
        q2 = q[:, h * hw + nope:(h + 1) * hw] * cs
        qr = q2 + pltpu.roll(q2, LANE // 2, axis=1)
        qr = jnp.where(lane < LANE // 2, qr, 0.0)
        q_ref[:, h * qw:h * qw + kvl] = (qlat * scale).astype(BF16)
        q_ref[:, h * qw + kvl:(h + 1) * qw] = (qr * scale).astype(BF16)


def _qproj(cq, wuq_ext, wukt, l, cs, *, heads, nope, kvl, scale):
    n, ql = cq.shape
    hw = nope + LANE
    qw = kvl + LANE
    tm = _tile(n, 512)
    body = functools.partial(_qproj_body, heads=heads, nope=nope, kvl=kvl, scale=scale)
    return pl.pallas_call(
        body, grid=(n // tm,),
        in_specs=[_rows(tm, ql), _layer(l, (ql, heads * hw)), _layer(l, (heads, nope, kvl)),
                  _rows(tm, LANE)],
        out_specs=_rows(tm, heads * qw),
        out_shape=jax.ShapeDtypeStruct((n, heads * qw), BF16),
        compiler_params=_params("parallel"), name="q_proj")(cq, wuq_ext, wukt, cs)


def _flash_body(q_ref, k_ref, o_ref, m_ref, l_ref, acc_ref, *, heads, kvl, qw):
    i = pl.program_id(1)
    j = pl.program_id(2)

    @pl.when(j == 0)
    def _():
        m_ref[...] = jnp.full(m_ref.shape, NEG, F32)
        l_ref[...] = jnp.zeros(l_ref.shape, F32)
        acc_ref[...] = jnp.zeros(acc_ref.shape, F32)

    def step(masked):
        k = k_ref[...]
        v = k[:, :kvl]
        for h in range(heads):
            s = _dot_nt(q_ref[:, h * qw:(h + 1) * qw], k)
            if masked:
                row = lax.broadcasted_iota(jnp.int32, s.shape, 0)
                col = lax.broadcasted_iota(jnp.int32, s.shape, 1)
                s = jnp.where(col <= row, s, NEG)
            m_prev = m_ref[h]
            m_new = jnp.maximum(m_prev, jnp.max(s, axis=1, keepdims=True))
            a = jnp.exp(m_prev - m_new)
            p = jnp.exp(s - m_new)
            l_ref[h] = a * l_ref[h] + jnp.sum(p, axis=1, keepdims=True)
            acc_ref[h] = a * acc_ref[h] + _dot(p.astype(BF16), v)
            m_ref[h] = m_new

    @pl.when(j < i)
    def _():
        step(False)

    @pl.when(j == i)
    def _():
        step(True)
        for h in range(heads):
            o_ref[:, h * kvl:(h + 1) * kvl] = (acc_ref[h] / l_ref[h]).astype(BF16)


def _flash(qcat, kcat, *, batch, seq, heads, kvl):
    qw = kvl + LANE
    tq = _tile(seq, 512)
    nq = seq // tq
    body = functools.partial(_flash_body, heads=heads, kvl=kvl, qw=qw)
    return pl.pallas_call(
        body, grid=(batch, nq, nq),
        in_specs=[pl.BlockSpec((tq, heads * qw), lambda b, i, j: (b * nq + i, 0)),
                  pl.BlockSpec((tq, qw), lambda b, i, j: (b * nq + jnp.minimum(i, j), 0))],
        out_specs=pl.BlockSpec((tq, heads * kvl), lambda b, i, j: (b * nq + i, 0)),
        out_shape=jax.ShapeDtypeStruct((batch * seq, heads * kvl), BF16),
        scratch_shapes=[pltpu.VMEM((heads, tq, 1), F32), pltpu.VMEM((heads, tq, 1), F32),
                        pltpu.VMEM((heads, tq, kvl), F32)],
        compiler_params=_params("parallel", "parallel", "arbitrary"), name="prompt_attn")(qcat, kcat)


def _paged_body(pt_ref, q_ref, kn_ref, ckv_hbm, kr_hbm, o_ref, ckv_buf, kr_buf, sem, s_ref, kb_ref,
                *, layer, n_pages, page, kvl, rope, ts, chunk_pages):
    b = pl.program_id(0)
    nb = pl.num_programs(0)
    slot = lax.rem(b, 2)

    def page_copies(bb, sl, p):
        pg = pt_ref[bb * n_pages + p]
        return (pltpu.make_async_copy(ckv_hbm.at[layer, pg], ckv_buf.at[sl, p], sem.at[sl, 0]),
                pltpu.make_async_copy(kr_hbm.at[layer, pg], kr_buf.at[sl, p], sem.at[sl, 1]))

    def fetch(bb, sl):
        def issue(p, c):
            for cp in page_copies(bb, sl, p):
                cp.start()
            return c
        lax.fori_loop(0, n_pages, issue, 0)

    @pl.when(b == 0)
    def _():
        fetch(0, 0)

    @pl.when(b + 1 < nb)
    def _():
        fetch(b + 1, 1 - slot)

    def drain(p, c):
        for cp in page_copies(b, slot, p):
            cp.wait()
        return c
    lax.fori_loop(0, n_pages, drain, 0)

    q = q_ref[0]
    ql = q[:, :kvl]
    qr = q[:, kvl:kvl + rope]
    ck = chunk_pages * page
    n_chunks = n_pages // chunk_pages
    for c in range(n_chunks):
        pages = pl.ds(c * chunk_pages, chunk_pages)
        kc = ckv_buf[slot, pages].reshape(ck, kvl).astype(BF16)
        rc = kr_buf[slot, pages].reshape(ck, rope).astype(BF16)
        s_ref[:, c * ck:(c + 1) * ck] = _dot_nt(ql, kc) + _dot_nt(qr, rc)
        kb_ref[c * ck:(c + 1) * ck, :] = kc

    kn = kn_ref[0]
    sn = _dot_nt(q, kn)
    qj = lax.broadcasted_iota(jnp.int32, sn.shape, 0) & (ts - 1)
    col = lax.broadcasted_iota(jnp.int32, sn.shape, 1)
    sn = jnp.where(col <= qj, sn, NEG)

    m = jnp.maximum(jnp.max(s_ref[...], axis=1, keepdims=True), jnp.max(sn, axis=1, keepdims=True))
    pn = jnp.exp(sn - m)
    l = jnp.sum(pn, axis=1, keepdims=True)
    acc = _dot(pn.astype(BF16), kn[:, :kvl])
    for c in range(n_chunks):
        p = jnp.exp(s_ref[:, c * ck:(c + 1) * ck] - m)
        l = l + jnp.sum(p, axis=1, keepdims=True)
        acc = acc + _dot(p.astype(BF16), kb_ref[c * ck:(c + 1) * ck, :])
    o_ref[0] = (acc / l).astype(BF16)


def _paged(page_table, q_s, k_new, cache_ckv, cache_krope, l, *, ts, kvl, rope):
    bs, rows, qw = q_s.shape
    n_pages = page_table.shape[1]
    page = cache_ckv.shape[2]
    chunk_pages = _tile(n_pages, 8, mult=1)
    past = n_pages * page
    assert ts & (ts - 1) == 0
    body = functools.partial(_paged_body, layer=l, n_pages=n_pages, page=page, kvl=kvl, rope=rope,
                             ts=ts, chunk_pages=chunk_pages)
    grid_spec = pltpu.PrefetchScalarGridSpec(
        num_scalar_prefetch=1, grid=(bs,),
        in_specs=[pl.BlockSpec((1, rows, qw), lambda b, pt: (b, 0, 0)),
                  pl.BlockSpec((1, k_new.shape[1], qw), lambda b, pt: (b, 0, 0)),
                  pl.BlockSpec(memory_space=pl.ANY), pl.BlockSpec(memory_space=pl.ANY)],
        out_specs=pl.BlockSpec((1, rows, kvl), lambda b, pt: (b, 0, 0)),
        scratch_shapes=[pltpu.VMEM((2, n_pages, page, kvl), F32),
                        pltpu.VMEM((2, n_pages, page, rope), F32),
                        pltpu.SemaphoreType.DMA((2, 2)),
                        pltpu.VMEM((rows, past), F32),
                        pltpu.VMEM((past, kvl), BF16)])
    return pl.pallas_call(
        body, grid_spec=grid_spec, out_shape=jax.ShapeDtypeStruct((bs, rows, kvl), BF16),
        compiler_params=_params("arbitrary"), name="paged_attn")(
            page_table.reshape(-1), q_s, k_new, cache_ckv, cache_krope)


def _s5_prep_body(lr_ref, li_ref, ldt_ref, br_ref, bi_ref, ar_ref, ai_ref, bbr_ref, bbi_ref):
    lr = lr_ref[...]
    li = li_ref[...]
    dt = jnp.exp(ldt_ref[...])
    mag = jnp.exp(lr * dt)
    ar = mag * jnp.cos(li * dt)
    ai = mag * jnp.sin(li * dt)
    den = lr * lr + li * li
    zr = ((ar - 1.0) * lr + ai * li) / den
    zi = (ai * lr - (ar - 1.0) * li) / den
    ar_ref[...] = ar
    ai_ref[...] = ai
    br = br_ref[...]
    bi = bi_ref[...]
    bbr_ref[...] = zr * br - zi * bi
    bbi_ref[...] = zr * bi + zi * br


def _s5_prep(lam_re, lam_im, log_dt, b_re_t, b_im_t):
    nl, g, _, p = lam_re.shape
    c = b_re_t.shape[2]
    gp = pl.BlockSpec((None, g, 1, p), lambda l: (l, 0, 0, 0))
    gcp = pl.BlockSpec((None, g, c, p), lambda l: (l, 0, 0, 0))
    return pl.pallas_call(
        _s5_prep_body, grid=(nl,),
        in_specs=[gp, gp, pl.BlockSpec((None, g, 1, 1), lambda l: (l, 0, 0, 0)), gcp, gcp],
        out_specs=[gp, gp, gcp, gcp],
        out_shape=[jax.ShapeDtypeStruct((nl, g, 1, p), F32)] * 2
        + [jax.ShapeDtypeStruct((nl, g, c, p), F32)] * 2,
        compiler_params=_params("parallel"), name="s5_prep")(lam_re, lam_im, log_dt, b_re_t, b_im_t)


def _s5_scan_body(u_ref, bre_ref, bim_ref, cre_ref, cim_ref, ar_ref, ai_ref, s0r_ref, s0i_ref,
                  y_ref, fr_ref, fi_ref, xr_ref, xi_ref, str_ref, sti_ref, *, bp, tc, nslab, sc, sp, lw):
    @pl.when(pl.program_id(0) == 0)
    def _():
        str_ref[...] = s0r_ref[...]
        sti_ref[...] = s0i_ref[...]

    ub = u_ref[...].astype(BF16)
    for j in range(nslab):
        uj = ub[:, j * sc:(j + 1) * sc]
        xr_ref[:, j * sp:(j + 1) * sp] = _dot(uj, bre_ref[j])
        xi_ref[:, j * sp:(j + 1) * sp] = _dot(uj, bim_ref[j])

    for p0 in range(0, nslab * sp, lw):
        lanes = slice(p0, p0 + lw)
        ar = jnp.broadcast_to(ar_ref[:, lanes], (bp, lw))
        ai = jnp.broadcast_to(ai_ref[:, lanes], (bp, lw))
        sr = str_ref[:, lanes]
        si = sti_ref[:, lanes]
        for t in range(tc):
            rows = slice(t * bp, (t + 1) * bp)
            nr = ar * sr - ai * si + xr_ref[rows, lanes]
            ni = ar * si + ai * sr + xi_ref[rows, lanes]
            xr_ref[rows, lanes] = nr
            xi_ref[rows, lanes] = ni
            sr, si = nr, ni
        str_ref[:, lanes] = sr
        sti_ref[:, lanes] = si

    for j in range(nslab):
        slab = slice(j * sp, (j + 1) * sp)
        y_ref[:, j * sc:(j + 1) * sc] = (_dot(xr_ref[:, slab].astype(BF16), cre_ref[j])
                                         - _dot(xi_ref[:, slab].astype(BF16), cim_ref[j]))
    fr_ref[...] = str_ref[...]
    fi_ref[...] = sti_ref[...]


def _s5_scan(u_tb, bp, blk, l, s0_re, s0_im):
    bre, bim, cre, cim, ar, ai = blk
    rows, w = u_tb.shape
    nslab, sc, sp = bre.shape[1:]
    gp = nslab * sp
    steps = rows // bp
    tc = _tile(steps, max(1, 256 // bp), mult=1)
    rc = tc * bp
    lw = max(LANE, (8 * SUBLANE * LANE) // bp)
    lw = _tile(gp, lw, mult=LANE)
    body = functools.partial(_s5_scan_body, bp=bp, tc=tc, nslab=nslab, sc=sc, sp=sp, lw=lw)
    return pl.pallas_call(
        body, grid=(steps // tc,),
        in_specs=[_rows(rc, w), _layer(l, (nslab, sc, sp)), _layer(l, (nslab, sc, sp)),
                  _layer(l, (nslab, sp, sc)), _layer(l, (nslab, sp, sc)),
                  _layer(l, (1, gp)), _layer(l, (1, gp)), _const((bp, gp)), _const((bp, gp))],
        out_specs=[_rows(rc, w), _const((bp, gp)), _const((bp, gp))],
        out_shape=[jax.ShapeDtypeStruct((rows, w), F32), jax.ShapeDtypeStruct((bp, gp), F32),
                   jax.ShapeDtypeStruct((bp, gp), F32)],
        scratch_shapes=[pltpu.VMEM((rc, gp), F32), pltpu.VMEM((rc, gp), F32),
                        pltpu.VMEM((bp, gp), F32), pltpu.VMEM((bp, gp), F32)],
        compiler_params=_params("arbitrary"), name="s5_scan")(
            u_tb, bre, bim, cre, cim, ar, ai, s0_re, s0_im)


def _s5_post_body(y_ref, u_ref, d_ref, wglu_ref, g_ref, o_ref):
    yf = y_ref[...] + d_ref[...] * u_ref[...]
    g = jax.nn.gelu(yf)
    z = g * jax.nn.sigmoid(_dot(g.astype(BF16), wglu_ref[...]))
    o_ref[...] = _rms(z, g_ref[...]).astype(BF16)


def _s5_post(y, u, d, wglu, g, l):
    n, w = y.shape
    tm = _tile(n, 512)
    return pl.pallas_call(
        _s5_post_body, grid=(n // tm,),
        in_specs=[_rows(tm, w), _rows(tm, w), _layer(l, (1, w)), _layer(l, (w, w)), _layer(l, (1, w))],
        out_specs=_rows(tm, w), out_shape=jax.ShapeDtypeStruct((n, w), BF16),
        compiler_params=_params("parallel"), name="s5_post")(y, u, d, wglu, g)


def _mixout_body(o_ref, ssm_ref, x_ref, wuv_ref, gm_ref, wout_ref, g1_ref, b1_ref, wmq_ref,
                 x1_ref, qm_ref, *, heads, kvl, ssm_w, alpha, mscale):
    mla = jnp.concatenate(
        [_dot(o_ref[:, h * kvl:(h + 1) * kvl], wuv_ref[h]) for h in range(heads)], axis=-1)
    mla_n = _rms(mla, gm_ref[...]).astype(BF16)
    m = _dot(ssm_ref[...], wout_ref[:ssm_w, :]) + _dot(mla_n, wout_ref[ssm_w:, :])
    x1 = _ln(alpha * x_ref[...] + m, g1_ref[...], b1_ref[...])
    x1_ref[...] = x1
    qm_ref[...] = (_dot(x1.astype(BF16), wmq_ref[...]) * mscale).astype(BF16)


def _mixout(o_lat, ssm_n, x, wuv, gm, wout, g1, b1, wmq, l, *, heads, kvl, alpha, mscale):
    n, d = x.shape
    ssm_w = ssm_n.shape[1]
    v = wuv.shape[3]
    mix = wout.shape[1]
    mq = wmq.shape[2]
    tm = _tile(n, 512)
    body = functools.partial(_mixout_body, heads=heads, kvl=kvl, ssm_w=ssm_w, alpha=alpha, mscale=mscale)
    return pl.pallas_call(
        body, grid=(n // tm,),
        in_specs=[_rows(tm, heads * kvl), _rows(tm, ssm_w), _rows(tm, d), _layer(l, (heads, kvl, v)),
                  _layer(l, (1, heads * v)), _layer(l, (mix, d)), _layer(l, (1, d)), _layer(l, (1, d)),
                  _layer(l, (d, mq))],
        out_specs=[_rows(tm, d), _rows(tm, mq)],
        out_shape=[jax.ShapeDtypeStruct((n, d), F32), jax.ShapeDtypeStruct((n, mq), BF16)],
        compiler_params=_params("parallel"), name="mix_out")(o_lat, ssm_n, x, wuv, gm, wout, g1, b1, wmq)


def _memkv_body(mem_ref, wk_ref, wv_ref, k_ref, v_ref):
    mb = mem_ref[...].astype(BF16)
    k_ref[...] = _dot(mb, wk_ref[...])
    v_ref[...] = _dot(mb, wv_ref[...])


def _memkv(mem, wk, wv):
    nl, d, hk = wk.shape
    rows = mem.shape[0]
    w = pl.BlockSpec((None, d, hk), lambda l: (l, 0, 0))
    o = pl.BlockSpec((None, rows, hk), lambda l: (l, 0, 0))
    return pl.pallas_call(
        _memkv_body, grid=(nl,), in_specs=[_const((rows, d)), w, w], out_specs=[o, o],
        out_shape=[jax.ShapeDtypeStruct((nl, rows, hk), F32)] * 2,
        compiler_params=_params("parallel"), name="mem_kv")(mem, wk, wv)


def _softmax_rows(s):
    p = jnp.exp(s - jnp.max(s, axis=1, keepdims=True))
    return p / jnp.sum(p, axis=1, keepdims=True)


def _cross_p_body(q_ref, mk_ref, mv_ref, o_ref, *, mh, md):
    mk = mk_ref[...].astype(BF16)
    mv = mv_ref[...].astype(BF16)
    for h in range(mh):
        cols = slice(h * md, (h + 1) * md)
        p = _softmax_rows(_dot_nt(q_ref[:, cols], mk[:, cols]))
        o_ref[:, cols] = _dot(p.astype(BF16), mv[:, cols]).astype(BF16)


def _cross_prompt(qm, memk, memv, l, *, batch, seq, mh, md):
    mlen = memk.shape[1] // batch
    hk = mh * md
    tm = _tile(seq, 512)
    nt = seq // tm
    kv = pl.BlockSpec((None, mlen, hk), lambda b, i: (l, b, 0))
    body = functools.partial(_cross_p_body, mh=mh, md=md)
    return pl.pallas_call(
        body, grid=(batch, nt),
        in_specs=[pl.BlockSpec((tm, hk), lambda b, i: (b * nt + i, 0)), kv, kv],
        out_specs=pl.BlockSpec((tm, hk), lambda b, i: (b * nt + i, 0)),
        out_shape=jax.ShapeDtypeStruct((batch * seq, hk), BF16),
        compiler_params=_params("parallel", "parallel"), name="cross_prompt")(qm, memk, memv)


def _cross_s_body(q_ref, mk_ref, mv_ref, o_ref, *, bb, ts, mh, md):
    q_all = q_ref[...].astype(F32)
    lane = lax.broadcasted_iota(jnp.int32, (ts, mh * md), 1)
    head_lanes = [(lane >= h * md) & (lane < (h + 1) * md) for h in range(mh)]
    outs = []
    for b in range(bb):
        q = q_all[b * ts:(b + 1) * ts, :]
        qe = jnp.concatenate([jnp.where(head_lanes[h], q, 0.0) for h in range(mh)], axis=0)
        p = _softmax_rows(_dot_nt(qe.astype(BF16), mk_ref[b].astype(BF16)))
        o = _dot(p.astype(BF16), mv_ref[b].astype(BF16))
        ob = jnp.where(head_lanes[0], o[:ts], 0.0)
        for h in range(1, mh):
            ob = ob + jnp.where(head_lanes[h], o[h * ts:(h + 1) * ts], 0.0)
        outs.append(ob)
    o_ref[...] = jnp.concatenate(outs, axis=0).astype(BF16)


def _cross_sample(qm_s, cache_k, cache_v, l, *, bs, ts, mh, md):
    mlen = cache_k.shape[2]
    hk = mh * md
    bb = _tile(bs, 8, mult=1)
    kv = pl.BlockSpec((None, bb, mlen, hk), lambda i: (l, i, 0, 0))
    body = functools.partial(_cross_s_body, bb=bb, ts=ts, mh=mh, md=md)
    return pl.pallas_call(
        body, grid=(bs // bb,),
        in_specs=[_rows(bb * ts, hk), kv, kv], out_specs=_rows(bb * ts, hk),
        out_shape=jax.ShapeDtypeStruct((bs * ts, hk), BF16),
        compiler_params=_params("parallel"), name="cross_sample")(qm_s, cache_k, cache_v)


def _top2(lt, n_exp):
    e = jnp.exp(lt - jnp.max(lt, axis=0, keepdims=True))
    probs = e / jnp.sum(e, axis=0, keepdims=True)
    eidx = lax.broadcasted_iota(jnp.int32, probs.shape, 0)
    m1 = jnp.max(probs, axis=0, keepdims=True)
    i1 = jnp.min(jnp.where(probs == m1, eidx, n_exp), axis=0, keepdims=True)
    rest = jnp.where(eidx == i1, -1.0, probs)
    m2 = jnp.max(rest, axis=0, keepdims=True)
    i2 = jnp.min(jnp.where(rest == m2, eidx, n_exp), axis=0, keepdims=True)
    den = m1 + m2
    return i1, i2, m1 / den, m2 / den


def _crossout_body(*refs, alpha, n_exp):
    if n_exp:
        o_ref, x1_ref, wmo_ref, g_ref, b_ref, wrh_ref, wrl_ref, x2_ref, x2b_ref, ti_ref, tg_ref = refs
    else:
        o_ref, x1_ref, wmo_ref, g_ref, b_ref, x2_ref, x2b_ref = refs
    x2 = _ln(alpha * x1_ref[...] + _dot(o_ref[...], wmo_ref[...]), g_ref[...], b_ref[...])
    x2_ref[...] = x2
    xh = x2.astype(BF16)
    x2b_ref[...] = xh
    if n_exp:
        xl = (x2 - xh.astype(F32)).astype(BF16)
        wh = wrh_ref[...]
        lt = _dot_nt(wh, xh) + (_dot_nt(wh, xl) + _dot_nt(wrl_ref[...], xh))
        i1, i2, g1, g2 = _top2(lt[:n_exp], n_exp)
        ti_ref[0:1, :] = i1
        ti_ref[1:2, :] = i2
        tg_ref[0:1, :] = g1
        tg_ref[1:2, :] = g2


def _crossout(o, x1, wmo, g, b, l, *, alpha, router=None):
    n, d = x1.shape
    hk = o.shape[1]
    tm = _tile(n, 512, mult=LANE)
    in_specs = [_rows(tm, hk), _rows(tm, d), _layer(l, (hk, d)), _layer(l, (1, d)), _layer(l, (1, d))]
    out_specs = [_rows(tm, d), _rows(tm, d)]
    out_shape = [jax.ShapeDtypeStruct((n, d), F32), jax.ShapeDtypeStruct((n, d), BF16)]
    args = [o, x1, wmo, g, b]
    n_exp = 0
    if router is not None:
        wrh, wrl, li, n_exp = router
        er = wrh.shape[1]
        in_specs += [_layer(li, (er, d)), _layer(li, (er, d))]
        top = pl.BlockSpec((2, tm), lambda i: (0, i))
        out_specs += [top, top]
        out_shape += [jax.ShapeDtypeStruct((2, n), jnp.int32), jax.ShapeDtypeStruct((2, n), F32)]
        args += [wrh, wrl]
    body = functools.partial(_crossout_body, alpha=alpha, n_exp=n_exp)
    return pl.pallas_call(
        body, grid=(n // tm,), in_specs=in_specs, out_specs=out_specs, out_shape=out_shape,
        compiler_params=_params("parallel"), name="cross_out")(*args)


def _ffn_body(te_ref, tv_ref, *refs, scaled):
    if scaled:
        x_ref, wg_ref, wu_ref, wd_ref, sc_ref, o_ref = refs
    else:
        x_ref, wg_ref, wu_ref, wd_ref, o_ref = refs
    i = pl.program_id(0)
    j = pl.program_id(1)
    last = pl.num_programs(1) - 1

    @pl.when(tv_ref[i] > 0)
    def _():
        x = x_ref[...]
        h = (jax.nn.silu(_dot(x, wg_ref[...])) * _dot(x, wu_ref[...])).astype(BF16)
        y = _dot(h, wd_ref[...])

        @pl.when(j == 0)
        def _():
            o_ref[...] = y

        @pl.when(j > 0)
        def _():
            o_ref[...] += y

        if scaled:
            @pl.when(j == last)
            def _():
                o_ref[...] *= sc_ref[...]

    @pl.when(tv_ref[i] == 0)
    def _():
        o_ref[...] = jnp.zeros(o_ref.shape, F32)


def _ffn(xs, wg, wu, wd, tile_expert, tile_valid, tm, tf, scale=None):
    rows, d = xs.shape
    ff = wg.shape[2]
    in_specs = [pl.BlockSpec((tm, d), lambda i, j, te, tv: (i, 0)),
                pl.BlockSpec((None, d, tf), lambda i, j, te, tv: (te[i], 0, j)),
                pl.BlockSpec((None, d, tf), lambda i, j, te, tv: (te[i], 0, j)),
                pl.BlockSpec((None, tf, d), lambda i, j, te, tv: (te[i], j, 0))]
    args = [xs, wg, wu, wd]
    if scale is not None:
        in_specs.append(pl.BlockSpec((tm, 1), lambda i, j, te, tv: (i, 0)))
        args.append(scale)
    grid_spec = pltpu.PrefetchScalarGridSpec(
        num_scalar_prefetch=2, grid=(rows // tm, ff // tf), in_specs=in_specs,
        out_specs=pl.BlockSpec((tm, d), lambda i, j, te, tv: (i, 0)))
    return pl.pallas_call(
        functools.partial(_ffn_body, scaled=scale is not None), grid_spec=grid_spec,
        out_shape=jax.ShapeDtypeStruct((rows, d), F32),
        compiler_params=_params("parallel", "arbitrary"), name="ffn")(tile_expert, tile_valid, *args)


def _addln_body(*refs, alpha, n_y):
    x_ref = refs[0]
    y_refs = refs[1:1 + n_y]
    g_ref, b_ref, o_ref = refs[1 + n_y:]
    y = y_refs[0][...]
    for r in y_refs[1:]:
        y = y + r[...]
    o_ref[...] = _ln(alpha * x_ref[...] + y, g_ref[...], b_ref[...])


def _addln(x, ys, g, b, l, *, alpha):
    n, d = x.shape
    tm = _tile(n, 512)
    body = functools.partial(_addln_body, alpha=alpha, n_y=len(ys))
    return pl.pallas_call(
        body, grid=(n // tm,),
        in_specs=[_rows(tm, d)] * (1 + len(ys)) + [_layer(l, (1, d)), _layer(l, (1, d))],
        out_specs=_rows(tm, d), out_shape=jax.ShapeDtypeStruct((n, d), F32),
        compiler_params=_params("parallel"), name="add_ln")(x, *ys, g, b)


def _pad_ff(w, axis, tf):
    ff = w.shape[axis]
    pad = (-ff) % tf
    if pad:
        widths = [(0, 0)] * w.ndim
        widths[axis] = (0, pad)
        w = jnp.pad(w, widths)
    return w


def _moe_plan(top_i, top_g, n_exp, tm):
    n = top_i.shape[1]
    flat_e = top_i.reshape(-1)
    onehot = (flat_e[:, None] == jnp.arange(n_exp, dtype=jnp.int32)[None, :]).astype(jnp.int32)
    csum = jnp.cumsum(onehot, axis=0)
    rank = jnp.sum(onehot * csum, axis=1) - 1
    counts = csum[-1]
    padded = ((counts + tm - 1) // tm) * tm
    ends = jnp.cumsum(padded)
    starts = ends - padded
    cstart = jnp.cumsum(counts) - counts
    pos = jnp.sum(onehot * starts[None, :], axis=1) + rank
    n_tiles = -(-2 * n // tm) + n_exp
    n_slots = n_tiles * tm
    order = jnp.argsort(flat_e, stable=True).astype(jnp.int32)
    tile_start = jnp.arange(n_tiles, dtype=jnp.int32) * tm
    tile_e = jnp.sum((tile_start[:, None] >= ends[None, :]).astype(jnp.int32), axis=1)
    tile_valid = (tile_e < n_exp).astype(jnp.int32)
    tile_e = jnp.minimum(tile_e, n_exp - 1)
    slot = jnp.arange(n_slots, dtype=jnp.int32)
    slot_e = jnp.repeat(tile_e, tm)
    eh = (slot_e[:, None] == jnp.arange(n_exp, dtype=jnp.int32)[None, :]).astype(jnp.int32)
    within = slot - jnp.sum(eh * starts[None, :], axis=1)
    live = (within < jnp.sum(eh * counts[None, :], axis=1)) & (jnp.repeat(tile_valid, tm) > 0)
    src = jnp.where(live, within + jnp.sum(eh * cstart[None, :], axis=1), 0)
    src_flat = jnp.take(order, src)
    src_token = jnp.where(live, src_flat % n, 0)
    slot_gate = jnp.where(live, jnp.take(top_g.reshape(-1), src_flat), 0.0)
    return src_token, slot_gate[:, None], tile_e, tile_valid, pos[:n], pos[n:]


def kernel(x_prompt, x_sample, mem_prompt, cache_ckv, cache_krope, cache_mem_k, cache_mem_v, state_ssm_re, state_ssm_im, page_table, w_in, g_q, w_uq, g_kv, w_uk, w_uv, ssm_lambda_re, ssm_lambda_im, ssm_log_dt, ssm_b_re, ssm_b_im, ssm_c_re, ssm_c_im, ssm_d, w_glu, g_ssm_out, g_mla_out, w_out, ln1_g, ln1_b, w_mq, w_mk, w_mv, w_mo, ln2_g, ln2_b, w_ff_gate, w_ff_up, w_ff_down, w_router, w_e_gate, w_e_up, w_e_down, ln3_g, ln3_b):
    bp_, tp, d = x_prompt.shape
    bs, ts, _ = x_sample.shape
    depth = w_in.shape[0]
    ssm_w = ssm_d.shape[1]
    groups, pstate = ssm_lambda_re.shape[1:]
    gch = ssm_b_re.shape[-1]
    ql = g_q.shape[1]
    kvl = g_kv.shape[1]
    heads = w_uq.shape[2]
    nope = w_uk.shape[3]
    rope = w_uq.shape[3] - nope
    vdim = w_uv.shape[3]
    mlen = mem_prompt.shape[1]
    mh, md = w_mq.shape[2:]
    n_exp = w_router.shape[2]
    n_pages, page = page_table.shape[1], cache_ckv.shape[2]
    past = n_pages * page
    n_p, n_s = bp_ * tp, bs * ts
    n = n_p + n_s
    assert 2 * rope == LANE and groups % SLAB_GROUPS == 0
    assert SLAB_GROUPS * gch == LANE and bp_ <= SUBLANE
    alpha = (2.0 * depth) ** 0.25
    att_scale = (nope + rope) ** -0.5
    mem_scale = md ** -0.5
    qw = kvl + LANE
    half = rope // 2

    o3 = ssm_w + ql + kvl
    w_in_ext = jnp.concatenate([w_in, w_in[:, :, o3 + half:o3 + rope], w_in[:, :, o3:o3 + half]],
                               axis=2).astype(BF16)
    uq_n, uq_r = w_uq[..., :nope], w_uq[..., nope:]
    w_uq_ext = jnp.concatenate([uq_n, uq_r, uq_r[..., half:], uq_r[..., :half]], axis=-1)
    w_uq_ext = w_uq_ext.reshape(depth, ql, heads * (nope + LANE)).astype(BF16)
    w_ukt = jnp.transpose(w_uk, (0, 2, 3, 1)).astype(BF16)
    w_uv_h = jnp.transpose(w_uv, (0, 2, 1, 3)).astype(BF16)
    w_glu_b = w_glu.astype(BF16)
    w_out_b = w_out.astype(BF16)
    w_mq_b = w_mq.reshape(depth, d, mh * md).astype(BF16)
    w_mk_b = w_mk.reshape(depth, d, mh * md).astype(BF16)
    w_mv_b = w_mv.reshape(depth, d, mh * md).astype(BF16)
    w_mo_b = w_mo.reshape(depth, mh * md, d).astype(BF16)
    row = lambda a: a[:, None, :]
    g_q_r, g_kv_r, d_r = row(g_q), row(g_kv), row(ssm_d)
    g_ssm_r, g_mla_r = row(g_ssm_out), row(g_mla_out)
    ln1g, ln1b, ln2g, ln2b, ln3g, ln3b = (row(a) for a in (ln1_g, ln1_b, ln2_g, ln2_b, ln3_g, ln3_b))

    tf_dense = _tile(w_ff_gate.shape[2], 512, mult=LANE)
    ffg, ffu, ffd = w_ff_gate.astype(BF16), w_ff_up.astype(BF16), w_ff_down.astype(BF16)
    eff = w_e_gate.shape[3]
    tf_moe = min(1024, -(-eff // LANE) * LANE)
    n_moe = w_e_gate.shape[0]
    eg = _pad_ff(w_e_gate.reshape(n_moe * n_exp, d, eff).astype(BF16), 2, tf_moe)
    eu = _pad_ff(w_e_up.reshape(n_moe * n_exp, d, eff).astype(BF16), 2, tf_moe)
    ed = _pad_ff(w_e_down.reshape(n_moe * n_exp, eff, d).astype(BF16), 1, tf_moe)
    er = -(-n_exp // 16) * 16
    wr_t = jnp.pad(jnp.transpose(w_router, (0, 2, 1)), ((0, 0), (0, er - n_exp), (0, 0)))
    wr_hi = wr_t.astype(BF16)
    wr_lo = (wr_t - wr_hi.astype(F32)).astype(BF16)

    a_re, a_im, bb_re, bb_im = _s5_prep(ssm_lambda_re[:, :, None, :], ssm_lambda_im[:, :, None, :],
                                        ssm_log_dt[:, :, None, None],
                                        jnp.swapaxes(ssm_b_re, 2, 3), jnp.swapaxes(ssm_b_im, 2, 3))
    nslab = groups // SLAB_GROUPS
    eye = jnp.eye(SLAB_GROUPS, dtype=F32)

    def b_blocks(bb):
        bb = bb.reshape(depth, nslab, SLAB_GROUPS, gch, pstate)
        return jnp.einsum('lsgcp,gh->lsgchp', bb, eye).reshape(
            depth, nslab, SLAB_GROUPS * gch, SLAB_GROUPS * pstate).astype(BF16)

    def c_blocks(c):
        c = c.reshape(depth, nslab, SLAB_GROUPS, gch, pstate)
        return jnp.einsum('lsgcp,gh->lshpgc', c, eye).reshape(
            depth, nslab, SLAB_GROUPS * pstate, SLAB_GROUPS * gch).astype(BF16)

    s5_blk = (b_blocks(bb_re), b_blocks(bb_im), c_blocks(ssm_c_re), c_blocks(ssm_c_im),
              a_re.reshape(depth, 1, groups * pstate), a_im.reshape(depth, 1, groups * pstate))

    x = jnp.concatenate([x_prompt.reshape(n_p, d), x_sample.reshape(n_s, d)], axis=0)
    pos = jnp.concatenate([jnp.tile(jnp.arange(tp, dtype=F32), bp_),
                           jnp.tile(past + jnp.arange(ts, dtype=F32), bs)])[:, None]
    cs = _rope_table(pos, rope)
    memk, memv = _memkv(mem_prompt.reshape(bp_ * mlen, d), w_mk_b, w_mv_b)
    cache_k = cache_mem_k.reshape(depth, bs, mlen, mh * md)
    cache_v = cache_mem_v.reshape(depth, bs, mlen, mh * md)
    zero_state = jnp.zeros((SUBLANE, groups * pstate), F32)
    kn_rows = LANE

    tm_ffn = _tile(n, 512)
    dense_te = lambda i: jnp.full((n // tm_ffn,), i, jnp.int32)
    dense_tv = jnp.ones((n // tm_ffn,), jnp.int32)

    outs = {k: [] for k in ("p_ckv", "p_kr", "p_sr", "p_si", "s_ckv", "s_kr", "s_sr", "s_si")}
    for l in range(depth):
        u, cq, kcat, ckv, kr = _inproj(x, w_in_ext, l, g_q_r, g_kv_r, cs,
                                       ssm_w=ssm_w, ql=ql, kvl=kvl, rope=rope)
        qcat = _qproj(cq, w_uq_ext, w_ukt, l, cs, heads=heads, nope=nope, kvl=kvl, scale=att_scale)

        o_p = _flash(qcat[:n_p], kcat[:n_p], batch=bp_, seq=tp, heads=heads, kvl=kvl)
        q_s = qcat[n_p:].reshape(bs, ts, heads, qw).transpose(0, 2, 1, 3).reshape(bs, heads * ts, qw)
        k_new = jnp.pad(kcat[n_p:].reshape(bs, ts, qw), ((0, 0), (0, kn_rows - ts), (0, 0)))
        o_s = _paged(page_table, q_s, k_new, cache_ckv, cache_krope, l, ts=ts, kvl=kvl, rope=rope)
        o_s = o_s.reshape(bs, heads, ts, kvl).transpose(0, 2, 1, 3).reshape(n_s, heads * kvl)
        o_lat = jnp.concatenate([o_p, o_s], axis=0)

        u_p = jnp.pad(u[:n_p].reshape(bp_, tp, ssm_w).transpose(1, 0, 2),
                      ((0, 0), (0, SUBLANE - bp_), (0, 0))).reshape(tp * SUBLANE, ssm_w)
        y_p, f_pr, f_pi = _s5_scan(u_p, SUBLANE, s5_blk, l, zero_state, zero_state)
        y_p = y_p.reshape(tp, SUBLANE, ssm_w)[:, :bp_].transpose(1, 0, 2).reshape(n_p, ssm_w)
        u_s = u[n_p:].reshape(bs, ts, ssm_w).transpose(1, 0, 2).reshape(n_s, ssm_w)
        y_s, f_sr, f_si = _s5_scan(u_s, bs, s5_blk, l, state_ssm_re[l].reshape(bs, -1),
                                   state_ssm_im[l].reshape(bs, -1))
        y_s = y_s.reshape(ts, bs, ssm_w).transpose(1, 0, 2).reshape(n_s, ssm_w)
        ssm_n = _s5_post(jnp.concatenate([y_p, y_s], axis=0), u, d_r, w_glu_b, g_ssm_r, l)

        x1, qm = _mixout(o_lat, ssm_n, x, w_uv_h, g_mla_r, w_out_b, ln1g, ln1b, w_mq_b, l,
                         heads=heads, kvl=kvl, alpha=alpha, mscale=mem_scale)

        c_p = _cross_prompt(qm[:n_p], memk, memv, l, batch=bp_, seq=tp, mh=mh, md=md)
        c_s = _cross_sample(qm[n_p:], cache_k, cache_v, l, bs=bs, ts=ts, mh=mh, md=md)
        c_o = jnp.concatenate([c_p, c_s], axis=0)

        if l % 2 == 0:
            x2, x2b = _crossout(c_o, x1, w_mo_b, ln2g, ln2b, l, alpha=alpha)
            y = _ffn(x2b, ffg, ffu, ffd, dense_te(l // 2), dense_tv, tm_ffn, tf_dense)
            x = _addln(x2, [y], ln3g, ln3b, l, alpha=alpha)
        else:
            x2, x2b, top_i, top_g = _crossout(c_o, x1, w_mo_b, ln2g, ln2b, l, alpha=alpha,
                                              router=(wr_hi, wr_lo, l // 2, n_exp))
            tm_moe = 512
            src, gate, tile_e, tile_v, pos1, pos2 = _moe_plan(top_i, top_g, n_exp, tm_moe)
            xs = jnp.take(x2b, src, axis=0)
            ys = _ffn(xs, eg, eu, ed, tile_e + (l // 2) * n_exp, tile_v, tm_moe, tf_moe, scale=gate)
            x = _addln(x2, [jnp.take(ys, pos1, axis=0), jnp.take(ys, pos2, axis=0)], ln3g, ln3b, l,
                       alpha=alpha)

        outs["p_ckv"].append(ckv[:n_p].reshape(bp_, tp, kvl))
        outs["p_kr"].append(kr[:n_p].reshape(bp_, tp, rope))
        outs["p_sr"].append(f_pr[:bp_].reshape(bp_, groups, pstate))
        outs["p_si"].append(f_pi[:bp_].reshape(bp_, groups, pstate))
        outs["s_ckv"].append(ckv[n_p:].reshape(bs, ts, kvl))
        outs["s_kr"].append(kr[n_p:].reshape(bs, ts, rope))
        outs["s_sr"].append(f_sr.reshape(bs, groups, pstate))
        outs["s_si"].append(f_si.reshape(bs, groups, pstate))

    st = {k: jnp.stack(v) for k, v in outs.items()}
    return (x[:n_p].reshape(bp_, tp, d), x[n_p:].reshape(bs, ts, d),
            st["p_ckv"], st["p_kr"], st["p_sr"], st["p_si"],
            memk.reshape(depth, bp_, mlen, mh, md), memv.reshape(depth, bp_, mlen, mh, md),
            st["s_ckv"], st["s_kr"], st["s_sr"], st["s_si"])
```

```python
import functools
import math

import jax
import jax.numpy as jnp
from jax import lax
from jax.experimental import pallas as pl
from jax.experimental.pallas import tpu as pltpu

F32 = jnp.float32
BF16 = jnp.bfloat16

LANE = 128
SUBLANE = 8
VMEM_LIMIT = 56 * 2 ** 20
LN_EPS = 1e-5
RMS_EPS = 1e-6
ROPE_BASE = 10000.0
SLAB_GROUPS = 8
NEG = float(jnp.finfo(jnp.float32).min)


def _dot(a, b):
    return jnp.dot(a, b, preferred_element_type=F32)


def _dot_nt(a, b):
    return lax.dot_general(a, b, (((1,), (1,)), ((), ())), preferred_element_type=F32)


def _ln(v, g, b):
    vc = v - jnp.mean(v, -1, keepdims=True)
    var = jnp.mean(vc * vc, -1, keepdims=True)
    return vc * lax.rsqrt(var + LN_EPS) * g + b


def _rms(v, g):
    return v * lax.rsqrt(jnp.mean(v * v, -1, keepdims=True) + RMS_EPS) * g


def _tile(n, pref, mult=SUBLANE):
    best = None
    for t in range(mult, min(n, pref) + 1, mult):
        if n % t == 0:
            best = t
    assert best is not None, (n, pref, mult)
    return best


def _params(*sem):
    return pltpu.CompilerParams(dimension_semantics=sem, vmem_limit_bytes=VMEM_LIMIT)


def _rows(tm, w):
    return pl.BlockSpec((tm, w), lambda i: (i, 0))


def _const(shape):
    return pl.BlockSpec(shape, lambda i: (0,) * len(shape))


def _layer(l, shape):
    return pl.BlockSpec((None,) + shape, lambda i: (l,) + (0,) * len(shape))


def _rope_table_body(pos_ref, inv_ref, cs_ref):
    ang = pos_ref[...] * inv_ref[...]
    lane = lax.broadcasted_iota(jnp.int32, ang.shape, 1)
    c = jnp.cos(ang)
    s = jnp.sin(ang)
    cs_ref[...] = jnp.where(lane < LANE // 2, c, jnp.where(lane < 3 * LANE // 4, -s, s))


def _rope_table(pos, rope):
    n = pos.shape[0]
    half = rope // 2
    inv = ROPE_BASE ** (-jnp.arange(half, dtype=F32) / half)
    inv4 = jnp.tile(inv, 4)[None, :]
    tm = _tile(n, 1024)
    return pl.pallas_call(
        _rope_table_body, grid=(n // tm,),
        in_specs=[_rows(tm, 1), _const((1, LANE))], out_specs=_rows(tm, LANE),
        out_shape=jax.ShapeDtypeStruct((n, LANE), F32), compiler_params=_params("parallel"),
        name="rope_table")(pos, inv4)


def _inproj_body(x_ref, w_ref, gq_ref, gkv_ref, cs_ref, u_ref, cq_ref, kcat_ref, ckv_ref, kr_ref,
                 ckvt_ref, *, ssm_w, ql, kvl, rope):
    xb = x_ref[...].astype(BF16)
    o1, o2, o3 = ssm_w, ssm_w + ql, ssm_w + ql + kvl
    u_ref[...] = _dot(xb, w_ref[:, :o1])
    cq_ref[...] = _rms(_dot(xb, w_ref[:, o1:o2]), gq_ref[...]).astype(BF16)
    ckv = _rms(_dot(xb, w_ref[:, o2:o3]), gkv_ref[...])
    ckv_ref[...] = ckv
    ckvt_ref[...] = jnp.transpose(ckv).astype(BF16)
    kr2 = _dot(xb, w_ref[:, o3:o3 + LANE]) * cs_ref[...]
    kr = kr2 + pltpu.roll(kr2, LANE // 2, axis=1)
    kr_ref[...] = kr[:, :rope]
    kcat_ref[:, :kvl] = ckv.astype(BF16)
    kcat_ref[:, kvl:] = kr.astype(BF16)


def _inproj(x, w_ext, l, gq, gkv, cs, *, ssm_w, ql, kvl, rope, tm):
    n, d = x.shape
    cols = w_ext.shape[2]
    assert tm % LANE == 0
    body = functools.partial(_inproj_body, ssm_w=ssm_w, ql=ql, kvl=kvl, rope=rope)
    return pl.pallas_call(
        body, grid=(n // tm,),
        in_specs=[_rows(tm, d), _layer(l, (d, cols)), _layer(l, (1, ql)), _layer(l, (1, kvl)),
                  _rows(tm, LANE)],
        out_specs=[_rows(tm, ssm_w), _rows(tm, ql), _rows(tm, kvl + LANE), _rows(tm, kvl),
                   _rows(tm, rope), pl.BlockSpec((kvl, tm), lambda i: (0, i))],
        out_shape=[jax.ShapeDtypeStruct((n, ssm_w), F32), jax.ShapeDtypeStruct((n, ql), BF16),
                   jax.ShapeDtypeStruct((n, kvl + LANE), BF16), jax.ShapeDtypeStruct((n, kvl), F32),
                   jax.ShapeDtypeStruct((n, rope), F32), jax.ShapeDtypeStruct((kvl, n), BF16)],
        compiler_params=_params("parallel"), name="in_proj")(x, w_ext, gq, gkv, cs)


def _qproj_body(cq_ref, wuq_ref, wukt_ref, cs_ref, q_ref, *, heads, nope, kvl, scale):
    q = _dot(cq_ref[...], wuq_ref[...])
    cs = cs_ref[...]
    lane = lax.broadcasted_iota(jnp.int32, cs.shape, 1)
    hw = nope + LANE
    qw = kvl + LANE
    for h in range(heads):
        qlat = _dot(q[:, h * hw:h * hw + nope].astype(BF16), wukt_ref[h])
        q2 = q[:, h * hw + nope:(h + 1) * hw] * cs
        qr = q2 + pltpu.roll(q2, LANE // 2, axis=1)
        qr = jnp.where(lane < LANE // 2, qr, 0.0)
        q_ref[:, h * qw:h * qw + kvl] = (qlat * scale).astype(BF16)
        q_ref[:, h * qw + kvl:(h + 1) * qw] = (qr * scale).astype(BF16)


def _qproj(cq, wuq_ext, wukt, l, cs, *, heads, nope, kvl, scale):
    n, ql = cq.shape
    hw = nope + LANE
    qw = kvl + LANE
    tm = _tile(n, 512)
    body = functools.partial(_qproj_body, heads=heads, nope=nope, kvl=kvl, scale=scale)
    return pl.pallas_call(
        body, grid=(n // tm,),
        in_specs=[_rows(tm, ql), _layer(l, (ql, heads * hw)), _layer(l, (heads, nope, kvl)),
                  _rows(tm, LANE)],
        out_specs=_rows(tm, heads * qw),
        out_shape=jax.ShapeDtypeStruct((n, heads * qw), BF16),
        compiler_params=_params("parallel"), name="q_proj")(cq, wuq_ext, wukt, cs)


def _flash_body(q_ref, k_ref, vt_ref, o_ref, m_ref, l_ref, acc_ref, *, heads, kvl, qw):
    i = pl.program_id(1)
    j = pl.program_id(2)

    @pl.when(j == 0)
    def _():
        m_ref[...] = jnp.full(m_ref.shape, NEG, F32)
        l_ref[...] = jnp.zeros(l_ref.shape, F32)
        acc_ref[...] = jnp.zeros(acc_ref.shape, F32)

    def step(masked):
        k = k_ref[...]
        vt = vt_ref[...]
        for h in range(heads):
            s = _dot_nt(k, q_ref[:, h * qw:(h + 1) * qw])
            if masked:
                key = lax.broadcasted_iota(jnp.int32, s.shape, 0)
                qry = lax.broadcasted_iota(jnp.int32, s.shape, 1)
                s = jnp.where(key <= qry, s, NEG)
            m_prev = m_ref[h]
            m_new = jnp.maximum(m_prev, jnp.max(s, axis=0, keepdims=True))
            a = jnp.exp(m_prev - m_new)
            p = jnp.exp(s - m_new)
            l_ref[h] = a * l_ref[h] + jnp.sum(p, axis=0, keepdims=True)
            acc_ref[h] = a * acc_ref[h] + _dot(vt, p.astype(BF16))
            m_ref[h] = m_new

    @pl.when(j < i)
    def _():
        step(False)

    @pl.when(j == i)
    def _():
        step(True)
        for h in range(heads):
            o_ref[:, h * kvl:(h + 1) * kvl] = jnp.transpose(acc_ref[h] / l_ref[h]).astype(BF16)


def _flash(qcat, kcat, ckv_t, *, batch, seq, heads, kvl):
    qw = kvl + LANE
    tq = _tile(seq, 512, mult=LANE)
    nq = seq // tq
    body = functools.partial(_flash_body, heads=heads, kvl=kvl, qw=qw)
    return pl.pallas_call(
        body, grid=(batch, nq, nq),
        in_specs=[pl.BlockSpec((tq, heads * qw), lambda b, i, j: (b * nq + i, 0)),
                  pl.BlockSpec((tq, qw), lambda b, i, j: (b * nq + jnp.minimum(i, j), 0)),
                  pl.BlockSpec((kvl, tq), lambda b, i, j: (0, b * nq + jnp.minimum(i, j)))],
        out_specs=pl.BlockSpec((tq, heads * kvl), lambda b, i, j: (b * nq + i, 0)),
        out_shape=jax.ShapeDtypeStruct((batch * seq, heads * kvl), BF16),
        scratch_shapes=[pltpu.VMEM((heads, 1, tq), F32), pltpu.VMEM((heads, 1, tq), F32),
                        pltpu.VMEM((heads, kvl, tq), F32)],
        compiler_params=_params("parallel", "parallel", "arbitrary"), name="prompt_attn")(
            qcat, kcat, ckv_t)


def _paged_body(pt_ref, q_ref, kn_ref, ckv_hbm, krt_hbm, o_ref, ckv_buf, krt_buf, sem, s_ref, kb_ref,
                *, layer, n_pages, page, kvl, rope, ts, chunk_pages):
    b = pl.program_id(0)
    nb = pl.num_programs(0)
    slot = lax.rem(b, 2)

    def fetch(bb, sl):
        def issue(p, c):
            pg = pt_ref[bb * n_pages + p]
            pltpu.make_async_copy(ckv_hbm.at[layer, pg], ckv_buf.at[sl, p], sem.at[sl, 0]).start()
            pltpu.make_async_copy(krt_hbm.at[layer, pg], krt_buf.at[sl, p], sem.at[sl, 1]).start()
            return c
        lax.fori_loop(0, n_pages, issue, 0, unroll=8)

    @pl.when(b == 0)
    def _():
        fetch(0, 0)

    @pl.when(b + 1 < nb)
    def _():
        fetch(b + 1, 1 - slot)

    pltpu.make_async_copy(ckv_buf.at[slot], ckv_buf.at[slot], sem.at[slot, 0]).wait()
    pltpu.make_async_copy(krt_buf.at[slot], krt_buf.at[slot], sem.at[slot, 1]).wait()

    q = q_ref[0]
    ql = q[:, :kvl]
    qr = q[:, kvl:kvl + rope]
    ck = chunk_pages * page
    n_chunks = n_pages // chunk_pages
    for c in range(n_chunks):
        kc = ckv_buf[slot, pl.ds(c * chunk_pages, chunk_pages)].reshape(ck, kvl).astype(BF16)
        rt = jnp.concatenate([krt_buf[slot, c * chunk_pages + p] for p in range(chunk_pages)],
                             axis=1).astype(BF16)
        s_ref[:, c * ck:(c + 1) * ck] = _dot_nt(ql, kc) + _dot(qr, rt)
        kb_ref[c * ck:(c + 1) * ck, :] = kc

    kn = kn_ref[0]
    sn = _dot_nt(q, kn)
    qj = lax.broadcasted_iota(jnp.int32, sn.shape, 0) & (ts - 1)
    col = lax.broadcasted_iota(jnp.int32, sn.shape, 1)
    sn = jnp.where(col <= qj, sn, NEG)

    m = jnp.maximum(jnp.max(s_ref[...], axis=1, keepdims=True), jnp.max(sn, axis=1, keepdims=True))
    pn = jnp.exp(sn - m)
    l = jnp.sum(pn, axis=1, keepdims=True)
    acc = _dot(pn.astype(BF16), kn[:, :kvl])
    for c in range(n_chunks):
        p = jnp.exp(s_ref[:, c * ck:(c + 1) * ck] - m)
        l = l + jnp.sum(p, axis=1, keepdims=True)
        acc = acc + _dot(p.astype(BF16), kb_ref[c * ck:(c + 1) * ck, :])
    o_ref[0] = (acc / l).astype(BF16)


def _paged(page_table, q_s, k_new, cache_ckv, cache_krope_t, l, *, ts, kvl, rope):
    bs, rows, qw = q_s.shape
    n_pages = page_table.shape[1]
    page = cache_ckv.shape[2]
    assert cache_krope_t.shape[2:] == (rope, page)
    chunk_pages = _tile(n_pages, 8, mult=1)
    past = n_pages * page
    assert ts & (ts - 1) == 0
    body = functools.partial(_paged_body, layer=l, n_pages=n_pages, page=page, kvl=kvl, rope=rope,
                             ts=ts, chunk_pages=chunk_pages)
    grid_spec = pltpu.PrefetchScalarGridSpec(
        num_scalar_prefetch=1, grid=(bs,),
        in_specs=[pl.BlockSpec((1, rows, qw), lambda b, pt: (b, 0, 0)),
                  pl.BlockSpec((1, k_new.shape[1], qw), lambda b, pt: (b, 0, 0)),
                  pl.BlockSpec(memory_space=pl.ANY), pl.BlockSpec(memory_space=pl.ANY)],
        out_specs=pl.BlockSpec((1, rows, kvl), lambda b, pt: (b, 0, 0)),
        scratch_shapes=[pltpu.VMEM((2, n_pages, page, kvl), F32),
                        pltpu.VMEM((2, n_pages, rope, page), F32),
                        pltpu.SemaphoreType.DMA((2, 2)),
                        pltpu.VMEM((rows, past), F32),
                        pltpu.VMEM((past, kvl), BF16)])
    return pl.pallas_call(
        body, grid_spec=grid_spec, out_shape=jax.ShapeDtypeStruct((bs, rows, kvl), BF16),
        compiler_params=_params("arbitrary"), name="paged_attn")(
            page_table.reshape(-1), q_s, k_new, cache_ckv, cache_krope_t)


def _s5_prep_body(lr_ref, li_ref, ldt_ref, br_ref, bi_ref, ar_ref, ai_ref, bbr_ref, bbi_ref):
    lr = lr_ref[...]
    li = li_ref[...]
    dt = jnp.exp(ldt_ref[...])
    mag = jnp.exp(lr * dt)
    ar = mag * jnp.cos(li * dt)
    ai = mag * jnp.sin(li * dt)
    den = lr * lr + li * li
    zr = ((ar - 1.0) * lr + ai * li) / den
    zi = (ai * lr - (ar - 1.0) * li) / den
    ar_ref[...] = ar
    ai_ref[...] = ai
    br = br_ref[...]
    bi = bi_ref[...]
    bbr_ref[...] = zr * br - zi * bi
    bbi_ref[...] = zr * bi + zi * br


def _s5_prep(lam_re, lam_im, log_dt, b_re_t, b_im_t):
    nl, g, _, p = lam_re.shape
    c = b_re_t.shape[2]
    gp = pl.BlockSpec((None, g, 1, p), lambda l: (l, 0, 0, 0))
    gcp = pl.BlockSpec((None, g, c, p), lambda l: (l, 0, 0, 0))
    return pl.pallas_call(
        _s5_prep_body, grid=(nl,),
        in_specs=[gp, gp, pl.BlockSpec((None, g, 1, 1), lambda l: (l, 0, 0, 0)), gcp, gcp],
        out_specs=[gp, gp, gcp, gcp],
        out_shape=[jax.ShapeDtypeStruct((nl, g, 1, p), F32)] * 2
        + [jax.ShapeDtypeStruct((nl, g, c, p), F32)] * 2,
        compiler_params=_params("parallel"), name="s5_prep")(lam_re, lam_im, log_dt, b_re_t, b_im_t)


def _swap_halves(z, nb):
    if 2 * nb == SUBLANE:
        return pltpu.roll(z, nb, axis=0)
    return jnp.concatenate([z[nb:], z[:nb]], axis=0)


def _s5_scan_body(u_ref, bw_ref, cw_ref, ar_ref, ai_ref, s0_ref, y_ref, f_ref, x_ref, st_ref,
                  *, nb, tc, nslab, sc, sp, lw):
    step_rows = 2 * nb
    w = nslab * sc

    @pl.when(pl.program_id(0) == 0)
    def _():
        st_ref[...] = s0_ref[...]

    for j in range(nslab):
        lhs = jnp.concatenate([u_ref[:, j * sc:(j + 1) * sc], u_ref[:, w + j * sc:w + (j + 1) * sc]], axis=1)
        x_ref[:, j * sp:(j + 1) * sp] = _dot(lhs, bw_ref[j])

    im_row = lax.broadcasted_iota(jnp.int32, (step_rows, lw), 0) >= nb
    for p0 in range(0, nslab * sp, lw):
        lanes = slice(p0, p0 + lw)
        a1 = jnp.broadcast_to(ar_ref[:, lanes], (step_rows, lw))
        ai = jnp.broadcast_to(ai_ref[:, lanes], (step_rows, lw))
        a2 = jnp.where(im_row, ai, -ai)
        z = st_ref[:, lanes]
        for t in range(tc):
            rows = slice(t * step_rows, (t + 1) * step_rows)
            z = a1 * z + a2 * _swap_halves(z, nb) + x_ref[rows, lanes]
            x_ref[rows, lanes] = z
        st_ref[:, lanes] = z

    re_row = (lax.broadcasted_iota(jnp.int32, (tc * step_rows, sc), 0) & nb) == 0
    for j in range(nslab):
        y2 = _dot(x_ref[:, j * sp:(j + 1) * sp].astype(BF16), cw_ref[j])
        y_ref[:, j * sc:(j + 1) * sc] = jnp.where(re_row, y2[:, :sc], y2[:, sc:])
    f_ref[...] = st_ref[...]


def _s5_scan(u2, nb, blk, l, s0):
    bw, cw, ar, ai = blk
    rows, w2 = u2.shape
    nslab, sc2, sp = bw.shape[1:]
    sc = sc2 // 2
    w = w2 // 2
    gp = nslab * sp
    step_rows = 2 * nb
    assert nb & (nb - 1) == 0 and step_rows % SUBLANE == 0
    steps = rows // step_rows
    tc = _tile(steps, max(1, 256 // step_rows), mult=1)
    rc = tc * step_rows
    lw = _tile(gp, max(LANE, (8 * SUBLANE * LANE) // step_rows), mult=LANE)
    body = functools.partial(_s5_scan_body, nb=nb, tc=tc, nslab=nslab, sc=sc, sp=sp, lw=lw)
    return pl.pallas_call(
        body, grid=(steps // tc,),
        in_specs=[_rows(rc, w2), _layer(l, (nslab, sc2, sp)), _layer(l, (nslab, sp, sc2)),
                  _layer(l, (1, gp)), _layer(l, (1, gp)), _const((step_rows, gp))],
        out_specs=[_rows(rc, w), _const((step_rows, gp))],
        out_shape=[jax.ShapeDtypeStruct((rows, w), F32), jax.ShapeDtypeStruct((step_rows, gp), F32)],
        scratch_shapes=[pltpu.VMEM((rc, gp), F32), pltpu.VMEM((step_rows, gp), F32)],
        compiler_params=_params("arbitrary"), name="s5_scan")(u2, bw, cw, ar, ai, s0)


def _s5_post_body(yre_ref, yim_ref, sre_ref, sim_ref, u_ref, d_ref, wglu_ref, g_ref, o_ref, *, npt):
    y = jnp.where(pl.program_id(0) < npt, yre_ref[...] + yim_ref[...], sre_ref[...] + sim_ref[...])
    yf = y + d_ref[...] * u_ref[...]
    g = jax.nn.gelu(yf)
    z = g * jax.nn.sigmoid(_dot(g.astype(BF16), wglu_ref[...]))
    o_ref[...] = _rms(z, g_ref[...]).astype(BF16)


def _s5_post(y2_p, y_s, u, d, wglu, g, l, *, batch, seq, nb, tm):
    n, w = u.shape
    ntt = seq // tm
    npt = batch * ntt

    def ymap(off):
        def index(i):
            ip = jnp.minimum(i, npt - 1)
            return (ip % ntt, off + ip // ntt)
        return pl.BlockSpec((tm, w), index)

    def smap(part):
        return pl.BlockSpec((None, tm, w), lambda i: (part, jnp.maximum(i - npt, 0), 0))

    return pl.pallas_call(
        functools.partial(_s5_post_body, npt=npt), grid=(n // tm,),
        in_specs=[ymap(0), ymap(nb), smap(0), smap(1), _rows(tm, w), _layer(l, (1, w)),
                  _layer(l, (w, w)), _layer(l, (1, w))],
        out_specs=_rows(tm, w), out_shape=jax.ShapeDtypeStruct((n, w), BF16),
        compiler_params=_params("parallel"), name="s5_post")(y2_p, y2_p, y_s, y_s, u, d, wglu, g)


def _split_rows(tm, w, npt):
    return (pl.BlockSpec((tm, w), lambda i: (jnp.minimum(i, npt - 1), 0)),
            pl.BlockSpec((tm, w), lambda i: (jnp.maximum(i - npt, 0), 0)))


def _mixout_body(op_ref, os_ref, ssm_ref, x_ref, wuv_ref, gm_ref, wout_ref, g1_ref, b1_ref, wmq_ref,
                 x1_ref, qm_ref, *, heads, kvl, ssm_w, alpha, mscale, npt):
    is_prompt = pl.program_id(0) < npt

    def o_head(h):
        cols = slice(h * kvl, (h + 1) * kvl)
        return jnp.where(is_prompt, op_ref[:, cols], os_ref[:, cols])

    mla = jnp.concatenate([_dot(o_head(h), wuv_ref[h]) for h in range(heads)], axis=-1)
    mla_n = _rms(mla, gm_ref[...]).astype(BF16)
    m = _dot(ssm_ref[...], wout_ref[:ssm_w, :]) + _dot(mla_n, wout_ref[ssm_w:, :])
    x1 = _ln(alpha * x_ref[...] + m, g1_ref[...], b1_ref[...])
    x1_ref[...] = x1
    qm_ref[...] = (_dot(x1.astype(BF16), wmq_ref[...]) * mscale).astype(BF16)


def _mixout(o_p, o_s, ssm_n, x, wuv, gm, wout, g1, b1, wmq, l, *, heads, kvl, alpha, mscale, tm):
    n, d = x.shape
    ssm_w = ssm_n.shape[1]
    v = wuv.shape[3]
    mix = wout.shape[1]
    mq = wmq.shape[2]
    npt = o_p.shape[0] // tm
    body = functools.partial(_mixout_body, heads=heads, kvl=kvl, ssm_w=ssm_w, alpha=alpha, mscale=mscale,
                             npt=npt)
    return pl.pallas_call(
        body, grid=(n // tm,),
        in_specs=[*_split_rows(tm, heads * kvl, npt), _rows(tm, ssm_w), _rows(tm, d),
                  _layer(l, (heads, kvl, v)), _layer(l, (1, heads * v)), _layer(l, (mix, d)),
                  _layer(l, (1, d)), _layer(l, (1, d)), _layer(l, (d, mq))],
        out_specs=[_rows(tm, d), _rows(tm, mq)],
        out_shape=[jax.ShapeDtypeStruct((n, d), F32), jax.ShapeDtypeStruct((n, mq), BF16)],
        compiler_params=_params("parallel"), name="mix_out")(
            o_p, o_s, ssm_n, x, wuv, gm, wout, g1, b1, wmq)


def _memkv_body(mem_ref, wk_ref, wv_ref, k_ref, v_ref):
    mb = mem_ref[...].astype(BF16)
    k_ref[...] = _dot(mb, wk_ref[...])
    v_ref[...] = _dot(mb, wv_ref[...])


def _memkv(mem, wk, wv):
    nl, d, hk = wk.shape
    rows = mem.shape[0]
    w = pl.BlockSpec((None, d, hk), lambda l: (l, 0, 0))
    o = pl.BlockSpec((None, rows, hk), lambda l: (l, 0, 0))
    return pl.pallas_call(
        _memkv_body, grid=(nl,), in_specs=[_const((rows, d)), w, w], out_specs=[o, o],
        out_shape=[jax.ShapeDtypeStruct((nl, rows, hk), F32)] * 2,
        compiler_params=_params("parallel"), name="mem_kv")(mem, wk, wv)


def _softmax_rows(s):
    p = jnp.exp(s - jnp.max(s, axis=1, keepdims=True))
    return p / jnp.sum(p, axis=1, keepdims=True)


def _cross_p_body(q_ref, mk_ref, mv_ref, o_ref, *, mh, md):
    mk = mk_ref[...].astype(BF16)
    mv = mv_ref[...].astype(BF16)
    for h in range(mh):
        cols = slice(h * md, (h + 1) * md)
        p = _softmax_rows(_dot_nt(q_ref[:, cols], mk[:, cols]))
        o_ref[:, cols] = _dot(p.astype(BF16), mv[:, cols]).astype(BF16)


def _cross_prompt(qm, memk, memv, l, *, batch, seq, mh, md):
    mlen = memk.shape[1] // batch
    hk = mh * md
    tm = _tile(seq, 512)
    nt = seq // tm
    kv = pl.BlockSpec((None, mlen, hk), lambda b, i: (l, b, 0))
    body = functools.partial(_cross_p_body, mh=mh, md=md)
    return pl.pallas_call(
        body, grid=(batch, nt),
        in_specs=[pl.BlockSpec((tm, hk), lambda b, i: (b * nt + i, 0)), kv, kv],
        out_specs=pl.BlockSpec((tm, hk), lambda b, i: (b * nt + i, 0)),
        out_shape=jax.ShapeDtypeStruct((batch * seq, hk), BF16),
        compiler_params=_params("parallel", "parallel"), name="cross_prompt")(qm, memk, memv)


def _cross_s_body(q_ref, mk_ref, mv_ref, o_ref, *, bb, ts, mh, md):
    q_all = q_ref[...].astype(F32)
    rows_kv = mk_ref.shape[1]
    row_head = lax.broadcasted_iota(jnp.int32, (mh * ts, rows_kv), 0) >> (ts.bit_length() - 1)
    col_head = lax.broadcasted_iota(jnp.int32, (mh * ts, rows_kv), 1) & (mh - 1)
    own = row_head == col_head
    outs = []
    for b in range(bb):
        q = q_all[b * ts:(b + 1) * ts, :]
        qh = jnp.concatenate([q[:, h * md:(h + 1) * md] for h in range(mh)], axis=0).astype(BF16)
        s = jnp.where(own, _dot_nt(qh, mk_ref[b].astype(BF16)), NEG)
        o = _dot(_softmax_rows(s).astype(BF16), mv_ref[b].astype(BF16))
        outs.append(jnp.concatenate([o[h * ts:(h + 1) * ts] for h in range(mh)], axis=1))
    o_ref[...] = jnp.concatenate(outs, axis=0).astype(BF16)


def _cross_sample(qm_s, cache_k, cache_v, l, *, bs, ts, mh, md):
    rows_kv = cache_k.shape[2]
    hk = mh * md
    assert mh & (mh - 1) == 0 and ts & (ts - 1) == 0
    bb = _tile(bs, 8, mult=1)
    kv = pl.BlockSpec((None, bb, rows_kv, md), lambda i: (l, i, 0, 0))
    body = functools.partial(_cross_s_body, bb=bb, ts=ts, mh=mh, md=md)
    return pl.pallas_call(
        body, grid=(bs // bb,),
        in_specs=[_rows(bb * ts, hk), kv, kv], out_specs=_rows(bb * ts, hk),
        out_shape=jax.ShapeDtypeStruct((bs * ts, hk), BF16),
        compiler_params=_params("parallel"), name="cross_sample")(qm_s, cache_k, cache_v)


def _top2(lt, n_exp):
    e = jnp.exp(lt - jnp.max(lt, axis=0, keepdims=True))
    probs = e / jnp.sum(e, axis=0, keepdims=True)
    eidx = lax.broadcasted_iota(jnp.int32, probs.shape, 0)
    m1 = jnp.max(probs, axis=0, keepdims=True)
    i1 = jnp.min(jnp.where(probs == m1, eidx, n_exp), axis=0, keepdims=True)
    rest = jnp.where(eidx == i1, -1.0, probs)
    m2 = jnp.max(rest, axis=0, keepdims=True)
    i2 = jnp.min(jnp.where(rest == m2, eidx, n_exp), axis=0, keepdims=True)
    den = m1 + m2
    return i1, i2, m1 / den, m2 / den


def _crossout_body(*refs, alpha, n_exp, npt):
    if n_exp:
        op_ref, os_ref, x1_ref, wmo_ref, g_ref, b_ref, wrh_ref, wrl_ref, x2_ref, ti_ref, tg_ref = refs
    else:
        op_ref, os_ref, x1_ref, wmo_ref, g_ref, b_ref, x2_ref = refs
    o = jnp.where(pl.program_id(0) < npt, op_ref[...], os_ref[...])
    x2 = _ln(alpha * x1_ref[...] + _dot(o, wmo_ref[...]), g_ref[...], b_ref[...])
    x2_ref[...] = x2
    if n_exp:
        xh = x2.astype(BF16)
        xl = (x2 - xh.astype(F32)).astype(BF16)
        wh = wrh_ref[...]
        lt = _dot_nt(wh, xh) + (_dot_nt(wh, xl) + _dot_nt(wrl_ref[...], xh))
        i1, i2, g1, g2 = _top2(lt[:n_exp], n_exp)
        ti_ref[0:1, :] = i1
        ti_ref[1:2, :] = i2
        tg_ref[0:1, :] = g1
        tg_ref[1:2, :] = g2


def _crossout(o_p, o_s, x1, wmo, g, b, l, *, alpha, tm, router=None):
    n, d = x1.shape
    hk = o_p.shape[1]
    assert tm % LANE == 0
    npt = o_p.shape[0] // tm
    in_specs = [*_split_rows(tm, hk, npt), _rows(tm, d), _layer(l, (hk, d)), _layer(l, (1, d)),
                _layer(l, (1, d))]
    out_specs = [_rows(tm, d)]
    out_shape = [jax.ShapeDtypeStruct((n, d), F32)]
    args = [o_p, o_s, x1, wmo, g, b]
    n_exp = 0
    if router is not None:
        wrh, wrl, li, n_exp = router
        er = wrh.shape[1]
        in_specs += [_layer(li, (er, d)), _layer(li, (er, d))]
        top = pl.BlockSpec((2, tm), lambda i: (0, i))
        out_specs += [top, top]
        out_shape += [jax.ShapeDtypeStruct((2, n), jnp.int32), jax.ShapeDtypeStruct((2, n), F32)]
        args += [wrh, wrl]
    body = functools.partial(_crossout_body, alpha=alpha, n_exp=n_exp, npt=npt)
    return pl.pallas_call(
        body, grid=(n // tm,), in_specs=in_specs, out_specs=out_specs, out_shape=out_shape,
        compiler_params=_params("parallel"), name="cross_out")(*args)


def _ffn_body(te_ref, tv_ref, *refs, scaled):
    if scaled:
        x_ref, wg_ref, wu_ref, wd_ref, sc_ref, o_ref, xb_ref = refs
    else:
        x_ref, wg_ref, wu_ref, wd_ref, o_ref, xb_ref = refs
    i = pl.program_id(0)
    j = pl.program_id(1)
    last = pl.num_programs(1) - 1

    @pl.when(tv_ref[i] > 0)
    def _():
        @pl.when(j == 0)
        def _():
            xb_ref[...] = x_ref[...].astype(BF16)

        x = xb_ref[...]
        h = (jax.nn.silu(_dot(x, wg_ref[...])) * _dot(x, wu_ref[...])).astype(BF16)
        y = _dot(h, wd_ref[...])

        @pl.when(j == 0)
        def _():
            o_ref[...] = y

        @pl.when(j > 0)
        def _():
            o_ref[...] += y

        if scaled:
            @pl.when(j == last)
            def _():
                o_ref[...] *= sc_ref[...]

    @pl.when(tv_ref[i] == 0)
    def _():
        o_ref[...] = jnp.zeros(o_ref.shape, F32)


def _ffn(xs, wg, wu, wd, tile_expert, tile_valid, tm, tf, scale=None):
    rows, d = xs.shape
    ff = wg.shape[2]
    in_specs = [pl.BlockSpec((tm, d), lambda i, j, te, tv: (i, 0)),
                pl.BlockSpec((None, d, tf), lambda i, j, te, tv: (te[i], 0, j)),
                pl.BlockSpec((None, d, tf), lambda i, j, te, tv: (te[i], 0, j)),
                pl.BlockSpec((None, tf, d), lambda i, j, te, tv: (te[i], j, 0))]
    args = [xs, wg, wu, wd]
    if scale is not None:
        in_specs.append(pl.BlockSpec((tm, 1), lambda i, j, te, tv: (i, 0)))
        args.append(scale)
    grid_spec = pltpu.PrefetchScalarGridSpec(
        num_scalar_prefetch=2, grid=(rows // tm, ff // tf), in_specs=in_specs,
        out_specs=pl.BlockSpec((tm, d), lambda i, j, te, tv: (i, 0)),
        scratch_shapes=[pltpu.VMEM((tm, d), BF16)])
    return pl.pallas_call(
        functools.partial(_ffn_body, scaled=scale is not None), grid_spec=grid_spec,
        out_shape=jax.ShapeDtypeStruct((rows, d), F32),
        compiler_params=_params("parallel", "arbitrary"), name="ffn")(tile_expert, tile_valid, *args)


def _addln_body(*refs, alpha, n_y):
    x_ref = refs[0]
    y_refs = refs[1:1 + n_y]
    g_ref, b_ref, o_ref = refs[1 + n_y:]
    y = y_refs[0][...]
    for r in y_refs[1:]:
        y = y + r[...]
    o_ref[...] = _ln(alpha * x_ref[...] + y, g_ref[...], b_ref[...])


def _addln(x, ys, g, b, l, *, alpha):
    n, d = x.shape
    tm = _tile(n, 512)
    body = functools.partial(_addln_body, alpha=alpha, n_y=len(ys))
    return pl.pallas_call(
        body, grid=(n // tm,),
        in_specs=[_rows(tm, d)] * (1 + len(ys)) + [_layer(l, (1, d)), _layer(l, (1, d))],
        out_specs=_rows(tm, d), out_shape=jax.ShapeDtypeStruct((n, d), F32),
        compiler_params=_params("parallel"), name="add_ln")(x, *ys, g, b)


def _cast_cols_body(w_ref, o_ref, *, f):
    o_ref[:, :f] = w_ref[...].astype(BF16)
    if o_ref.shape[1] > f:
        o_ref[:, f:] = jnp.zeros((o_ref.shape[0], o_ref.shape[1] - f), BF16)


def _cast_pad_cols(w, fp):
    e, d, f = w.shape
    td = _tile(d, 256)
    return pl.pallas_call(
        functools.partial(_cast_cols_body, f=f), grid=(e, d // td),
        in_specs=[pl.BlockSpec((None, td, f), lambda a, b: (a, b, 0))],
        out_specs=pl.BlockSpec((None, td, fp), lambda a, b: (a, b, 0)),
        out_shape=jax.ShapeDtypeStruct((e, d, fp), BF16),
        compiler_params=_params("parallel", "parallel"), name="cast_cols")(w)


def _cast_rows_body(w_ref, o_ref, *, nvalid):
    j = pl.program_id(1)

    @pl.when(j < nvalid)
    def _():
        o_ref[...] = w_ref[...].astype(BF16)

    @pl.when(j >= nvalid)
    def _():
        o_ref[...] = jnp.zeros(o_ref.shape, BF16)


def _cast_pad_rows(w, fp):
    e, f, d = w.shape
    tr = _tile(math.gcd(f, fp), 512, mult=2 * SUBLANE)
    nvalid = f // tr
    return pl.pallas_call(
        functools.partial(_cast_rows_body, nvalid=nvalid), grid=(e, fp // tr),
        in_specs=[pl.BlockSpec((None, tr, d), lambda a, b: (a, jnp.minimum(b, nvalid - 1), 0))],
        out_specs=pl.BlockSpec((None, tr, d), lambda a, b: (a, b, 0)),
        out_shape=jax.ShapeDtypeStruct((e, fp, d), BF16),
        compiler_params=_params("parallel", "parallel"), name="cast_rows")(w)


def _moe_plan(top_i, top_g, n_exp, tm):
    n = top_i.shape[1]
    flat_e = top_i.reshape(-1)
    onehot = (flat_e[:, None] == jnp.arange(n_exp, dtype=jnp.int32)[None, :]).astype(jnp.int32)
    csum = jnp.cumsum(onehot, axis=0)
    rank = jnp.sum(onehot * csum, axis=1) - 1
    counts = csum[-1]
    padded = ((counts + tm - 1) // tm) * tm
    ends = jnp.cumsum(padded)
    starts = ends - padded
    cstart = jnp.cumsum(counts) - counts
    pos = jnp.sum(onehot * starts[None, :], axis=1) + rank
    n_tiles = -(-2 * n // tm) + n_exp
    n_slots = n_tiles * tm
    order = jnp.argsort(flat_e, stable=True).astype(jnp.int32)
    tile_start = jnp.arange(n_tiles, dtype=jnp.int32) * tm
    tile_e = jnp.sum((tile_start[:, None] >= ends[None, :]).astype(jnp.int32), axis=1)
    tile_valid = (tile_e < n_exp).astype(jnp.int32)
    tile_e = jnp.minimum(tile_e, n_exp - 1)
    slot = jnp.arange(n_slots, dtype=jnp.int32)
    slot_e = jnp.repeat(tile_e, tm)
    eh = (slot_e[:, None] == jnp.arange(n_exp, dtype=jnp.int32)[None, :]).astype(jnp.int32)
    within = slot - jnp.sum(eh * starts[None, :], axis=1)
    live = (within < jnp.sum(eh * counts[None, :], axis=1)) & (jnp.repeat(tile_valid, tm) > 0)
    src = jnp.where(live, within + jnp.sum(eh * cstart[None, :], axis=1), 0)
    src_flat = jnp.take(order, src)
    src_token = jnp.where(live, src_flat % n, 0)
    slot_gate = jnp.where(live, jnp.take(top_g.reshape(-1), src_flat), 0.0)
    return src_token, slot_gate[:, None], tile_e, tile_valid, pos[:n], pos[n:]


def kernel(x_prompt, x_sample, mem_prompt, cache_ckv, cache_krope, cache_mem_k, cache_mem_v, state_ssm_re, state_ssm_im, page_table, w_in, g_q, w_uq, g_kv, w_uk, w_uv, ssm_lambda_re, ssm_lambda_im, ssm_log_dt, ssm_b_re, ssm_b_im, ssm_c_re, ssm_c_im, ssm_d, w_glu, g_ssm_out, g_mla_out, w_out, ln1_g, ln1_b, w_mq, w_mk, w_mv, w_mo, ln2_g, ln2_b, w_ff_gate, w_ff_up, w_ff_down, w_router, w_e_gate, w_e_up, w_e_down, ln3_g, ln3_b):
    bp_, tp, d = x_prompt.shape
    bs, ts, _ = x_sample.shape
    depth = w_in.shape[0]
    ssm_w = ssm_d.shape[1]
    groups, pstate = ssm_lambda_re.shape[1:]
    gch = ssm_b_re.shape[-1]
    ql = g_q.shape[1]
    kvl = g_kv.shape[1]
    heads = w_uq.shape[2]
    nope = w_uk.shape[3]
    rope = w_uq.shape[3] - nope
    vdim = w_uv.shape[3]
    mlen = mem_prompt.shape[1]
    mh, md = w_mq.shape[2:]
    n_exp = w_router.shape[2]
    n_pages, page = page_table.shape[1], cache_ckv.shape[2]
    past = n_pages * page
    n_p, n_s = bp_ * tp, bs * ts
    n = n_p + n_s
    assert 2 * rope == LANE and groups % SLAB_GROUPS == 0
    assert SLAB_GROUPS * gch == LANE and bp_ <= SUBLANE // 2
    alpha = (2.0 * depth) ** 0.25
    att_scale = (nope + rope) ** -0.5
    mem_scale = md ** -0.5
    qw = kvl + LANE
    half = rope // 2

    o3 = ssm_w + ql + kvl
    w_in_ext = jnp.concatenate([w_in, w_in[:, :, o3 + half:o3 + rope], w_in[:, :, o3:o3 + half]],
                               axis=2).astype(BF16)
    uq_n, uq_r = w_uq[..., :nope], w_uq[..., nope:]
    w_uq_ext = jnp.concatenate([uq_n, uq_r, uq_r[..., half:], uq_r[..., :half]], axis=-1)
    w_uq_ext = w_uq_ext.reshape(depth, ql, heads * (nope + LANE)).astype(BF16)
    w_ukt = jnp.transpose(w_uk, (0, 2, 3, 1)).astype(BF16)
    w_uv_h = jnp.transpose(w_uv, (0, 2, 1, 3)).astype(BF16)
    w_glu_b = w_glu.astype(BF16)
    w_out_b = w_out.astype(BF16)
    w_mq_b = w_mq.reshape(depth, d, mh * md).astype(BF16)
    w_mk_b = w_mk.reshape(depth, d, mh * md).astype(BF16)
    w_mv_b = w_mv.reshape(depth, d, mh * md).astype(BF16)
    w_mo_b = w_mo.reshape(depth, mh * md, d).astype(BF16)
    row = lambda a: a[:, None, :]
    g_q_r, g_kv_r, d_r = row(g_q), row(g_kv), row(ssm_d)
    g_ssm_r, g_mla_r = row(g_ssm_out), row(g_mla_out)
    ln1g, ln1b, ln2g, ln2b, ln3g, ln3b = (row(a) for a in (ln1_g, ln1_b, ln2_g, ln2_b, ln3_g, ln3_b))

    tf_dense = _tile(w_ff_gate.shape[2], 512, mult=LANE)
    dff = w_ff_gate.shape[2]
    ffg, ffu, ffd = _cast_pad_cols(w_ff_gate, dff), _cast_pad_cols(w_ff_up, dff), _cast_pad_rows(w_ff_down, dff)
    eff = w_e_gate.shape[3]
    tf_moe = min(1024, -(-eff // LANE) * LANE)
    effp = -(-eff // tf_moe) * tf_moe
    n_moe = w_e_gate.shape[0]
    eg = _cast_pad_cols(w_e_gate.reshape(n_moe * n_exp, d, eff), effp)
    eu = _cast_pad_cols(w_e_up.reshape(n_moe * n_exp, d, eff), effp)
    ed = _cast_pad_rows(w_e_down.reshape(n_moe * n_exp, eff, d), effp)
    er = -(-n_exp // 16) * 16
    wr_t = jnp.pad(jnp.transpose(w_router, (0, 2, 1)), ((0, 0), (0, er - n_exp), (0, 0)))
    wr_hi = wr_t.astype(BF16)
    wr_lo = (wr_t - wr_hi.astype(F32)).astype(BF16)

    a_re, a_im, bb_re, bb_im = _s5_prep(ssm_lambda_re[:, :, None, :], ssm_lambda_im[:, :, None, :],
                                        ssm_log_dt[:, :, None, None],
                                        jnp.swapaxes(ssm_b_re, 2, 3), jnp.swapaxes(ssm_b_im, 2, 3))
    nslab = groups // SLAB_GROUPS
    eye = jnp.eye(SLAB_GROUPS, dtype=F32)

    def b_blocks(bb):
        bb = bb.reshape(depth, nslab, SLAB_GROUPS, gch, pstate)
        return jnp.einsum('lsgcp,gh->lsgchp', bb, eye).reshape(
            depth, nslab, SLAB_GROUPS * gch, SLAB_GROUPS * pstate).astype(BF16)

    def c_blocks(c):
        c = c.reshape(depth, nslab, SLAB_GROUPS, gch, pstate)
        return jnp.einsum('lsgcp,gh->lshpgc', c, eye).reshape(
            depth, nslab, SLAB_GROUPS * pstate, SLAB_GROUPS * gch).astype(BF16)

    s5_blk = (jnp.concatenate([b_blocks(bb_re), b_blocks(bb_im)], axis=2),
              jnp.concatenate([c_blocks(ssm_c_re), -c_blocks(ssm_c_im)], axis=3),
              a_re.reshape(depth, 1, groups * pstate), a_im.reshape(depth, 1, groups * pstate))

    x = jnp.concatenate([x_prompt.reshape(n_p, d), x_sample.reshape(n_s, d)], axis=0)
    pos = jnp.concatenate([jnp.tile(jnp.arange(tp, dtype=F32), bp_),
                           jnp.tile(past + jnp.arange(ts, dtype=F32), bs)])[:, None]
    cs = _rope_table(pos, rope)
    memk, memv = _memkv(mem_prompt.reshape(bp_ * mlen, d), w_mk_b, w_mv_b)
    cache_k = cache_mem_k.reshape(depth, bs, mlen * mh, md)
    cache_v = cache_mem_v.reshape(depth, bs, mlen * mh, md)
    cache_krope_t = jnp.swapaxes(cache_krope, 2, 3)
    nbp = SUBLANE // 2
    zero_state = jnp.zeros((2 * nbp, groups * pstate), F32)
    kn_rows = LANE
    tm_tok = _tile(math.gcd(tp, n_s), 512)

    def scan_rows(u_bt):
        u_t = jnp.swapaxes(u_bt, 0, 1).astype(BF16)
        zz = jnp.zeros_like(u_t)
        rows = jnp.stack([jnp.concatenate([u_t, zz], -1), jnp.concatenate([zz, u_t], -1)], axis=1)
        return rows.reshape(-1, 2 * ssm_w)

    tm_ffn = _tile(n, 512)
    dense_te = lambda i: jnp.full((n // tm_ffn,), i, jnp.int32)
    dense_tv = jnp.ones((n // tm_ffn,), jnp.int32)

    outs = {k: [] for k in ("p_ckv", "p_kr", "p_sr", "p_si", "s_ckv", "s_kr", "s_sr", "s_si")}
    for l in range(depth):
        u, cq, kcat, ckv, kr, ckv_t = _inproj(x, w_in_ext, l, g_q_r, g_kv_r, cs,
                                              ssm_w=ssm_w, ql=ql, kvl=kvl, rope=rope, tm=tm_tok)
        qcat = _qproj(cq, w_uq_ext, w_ukt, l, cs, heads=heads, nope=nope, kvl=kvl, scale=att_scale)

        o_p = _flash(qcat, kcat, ckv_t, batch=bp_, seq=tp, heads=heads, kvl=kvl)
        q_s = qcat[n_p:].reshape(bs, ts, heads, qw).transpose(0, 2, 1, 3).reshape(bs, heads * ts, qw)
        k_new = jnp.pad(kcat[n_p:].reshape(bs, ts, qw), ((0, 0), (0, kn_rows - ts), (0, 0)))
        o_s = _paged(page_table, q_s, k_new, cache_ckv, cache_krope_t, l, ts=ts, kvl=kvl, rope=rope)
        o_s = o_s.reshape(bs, heads, ts, kvl).transpose(0, 2, 1, 3).reshape(n_s, heads * kvl)

        u_p = jnp.pad(u[:n_p].reshape(bp_, tp, ssm_w), ((0, nbp - bp_), (0, 0), (0, 0)))
        y2_p, f_p = _s5_scan(scan_rows(u_p), nbp, s5_blk, l, zero_state)
        s0_s = jnp.concatenate([state_ssm_re[l].reshape(bs, -1), state_ssm_im[l].reshape(bs, -1)], axis=0)
        y2_s, f_s = _s5_scan(scan_rows(u[n_p:].reshape(bs, ts, ssm_w)), bs, s5_blk, l, s0_s)
        y2_s = y2_s.reshape(ts, 2, bs, ssm_w).transpose(1, 2, 0, 3).reshape(2, n_s, ssm_w)
        ssm_n = _s5_post(y2_p.reshape(tp, 2 * nbp * ssm_w), y2_s, u, d_r, w_glu_b, g_ssm_r, l,
                         batch=bp_, seq=tp, nb=nbp, tm=tm_tok)

        x1, qm = _mixout(o_p, o_s, ssm_n, x, w_uv_h, g_mla_r, w_out_b, ln1g, ln1b, w_mq_b, l,
                         heads=heads, kvl=kvl, alpha=alpha, mscale=mem_scale, tm=tm_tok)

        c_p = _cross_prompt(qm, memk, memv, l, batch=bp_, seq=tp, mh=mh, md=md)
        c_s = _cross_sample(qm[n_p:], cache_k, cache_v, l, bs=bs, ts=ts, mh=mh, md=md)

        if l % 2 == 0:
            x2, = _crossout(c_p, c_s, x1, w_mo_b, ln2g, ln2b, l, alpha=alpha, tm=tm_tok)
            y = _ffn(x2, ffg, ffu, ffd, dense_te(l // 2), dense_tv, tm_ffn, tf_dense)
            x = _addln(x2, [y], ln3g, ln3b, l, alpha=alpha)
        else:
            x2, top_i, top_g = _crossout(c_p, c_s, x1, w_mo_b, ln2g, ln2b, l, alpha=alpha, tm=tm_tok,
                                         router=(wr_hi, wr_lo, l // 2, n_exp))
            tm_moe = 512
            src, gate, tile_e, tile_v, pos1, pos2 = _moe_plan(top_i, top_g, n_exp, tm_moe)
            take = lambda a, idx: a.at[idx].get(mode="promise_in_bounds")
            ys = _ffn(take(x2, src), eg, eu, ed, tile_e + (l // 2) * n_exp, tile_v, tm_moe, tf_moe,
                      scale=gate)
            x = _addln(x2, [take(ys, pos1), take(ys, pos2)], ln3g, ln3b, l, alpha=alpha)

        outs["p_ckv"].append(ckv[:n_p].reshape(bp_, tp, kvl))
        outs["p_kr"].append(kr[:n_p].reshape(bp_, tp, rope))
        outs["p_sr"].append(f_p[:bp_].reshape(bp_, groups, pstate))
        outs["p_si"].append(f_p[nbp:nbp + bp_].reshape(bp_, groups, pstate))
        outs["s_ckv"].append(ckv[n_p:].reshape(bs, ts, kvl))
        outs["s_kr"].append(kr[n_p:].reshape(bs, ts, rope))
        outs["s_sr"].append(f_s[:bs].reshape(bs, groups, pstate))
        outs["s_si"].append(f_s[bs:].reshape(bs, groups, pstate))

    st = {k: jnp.stack(v) for k, v in outs.items()}
    return (x[:n_p].reshape(bp_, tp, d), x[n_p:].reshape(bs, ts, d),
            st["p_ckv"], st["p_kr"], st["p_sr"], st["p_si"],
            memk.reshape(depth, bp_, mlen, mh, md), memv.reshape(depth, bp_, mlen, mh, md),
            st["s_ckv"], st["s_kr"], st["s_sr"], st["s_si"])
```

```python
import functools
import math

import jax
import jax.numpy as jnp
from jax import lax
from jax.experimental import pallas as pl
from jax.experimental.pallas import tpu as pltpu

F32 = jnp.float32
BF16 = jnp.bfloat16

LANE = 128
SUBLANE = 8
VMEM_LIMIT = 56 * 2 ** 20
LN_EPS = 1e-5
RMS_EPS = 1e-6
ROPE_BASE = 10000.0
SLAB_GROUPS = 8
NEG = float(jnp.finfo(jnp.float32).min)


def _dot(a, b):
    return jnp.dot(a, b, preferred_element_type=F32)


def _dot_nt(a, b):
    return lax.dot_general(a, b, (((1,), (1,)), ((), ())), preferred_element_type=F32)


def _ln(v, g, b):
    vc = v - jnp.mean(v, -1, keepdims=True)
    var = jnp.mean(vc * vc, -1, keepdims=True)
    return vc * lax.rsqrt(var + LN_EPS) * g + b


def _rms(v, g):
    return v * lax.rsqrt(jnp.mean(v * v, -1, keepdims=True) + RMS_EPS) * g


def _tile(n, pref, mult=SUBLANE):
    best = None
    for t in range(mult, min(n, pref) + 1, mult):
        if n % t == 0:
            best = t
    assert best is not None, (n, pref, mult)
    return best


def _params(*sem):
    return pltpu.CompilerParams(dimension_semantics=sem, vmem_limit_bytes=VMEM_LIMIT)


def _rows(tm, w):
    return pl.BlockSpec((tm, w), lambda i: (i, 0))


def _const(shape):
    return pl.BlockSpec(shape, lambda i: (0,) * len(shape))


def _layer(l, shape):
    return pl.BlockSpec((None,) + shape, lambda i: (l,) + (0,) * len(shape))


def _rope_table_body(pos_ref, inv_ref, cs_ref):
    ang = pos_ref[...] * inv_ref[...]
    lane = lax.broadcasted_iota(jnp.int32, ang.shape, 1)
    c = jnp.cos(ang)
    s = jnp.sin(ang)
    cs_ref[...] = jnp.where(lane < LANE // 2, c, jnp.where(lane < 3 * LANE // 4, -s, s))


def _rope_table(pos, rope):
    n = pos.shape[0]
    half = rope // 2
    inv = ROPE_BASE ** (-jnp.arange(half, dtype=F32) / half)
    inv4 = jnp.tile(inv, 4)[None, :]
    tm = _tile(n, 1024)
    return pl.pallas_call(
        _rope_table_body, grid=(n // tm,),
        in_specs=[_rows(tm, 1), _const((1, LANE))], out_specs=_rows(tm, LANE),
        out_shape=jax.ShapeDtypeStruct((n, LANE), F32), compiler_params=_params("parallel"),
        name="rope_table")(pos, inv4)


def _inproj_body(*refs, ssm_w, ql, kvl, rope, n_y, alpha, npt):
    u_ref, cq_ref, kcat_ref, ckv_ref, kr_ref, ckvt_ref = refs[-6:]
    if npt:
        xp_ref, xs_ref, w_ref, gq_ref, gkv_ref, cs_ref, xo_ref = refs[:-6]
        x = jnp.where(pl.program_id(0) < npt, xp_ref[...], xs_ref[...])
        xo_ref[...] = x
    elif n_y:
        x_ref = refs[0]
        g_ref, b_ref, w_ref, gq_ref, gkv_ref, cs_ref, xo_ref = refs[1 + n_y:-6]
        y = refs[1][...]
        for r in refs[2:1 + n_y]:
            y = y + r[...]
        x = _ln(alpha * x_ref[...] + y, g_ref[...], b_ref[...])
        xo_ref[...] = x
    else:
        x_ref, w_ref, gq_ref, gkv_ref, cs_ref = refs[:-6]
        x = x_ref[...]
    xb = x.astype(BF16)
    o1, o2, o3 = ssm_w, ssm_w + ql, ssm_w + ql + kvl
    u_ref[...] = _dot(xb, w_ref[:, :o1])
    cq_ref[...] = _rms(_dot(xb, w_ref[:, o1:o2]), gq_ref[...]).astype(BF16)
    ckv = _rms(_dot(xb, w_ref[:, o2:o3]), gkv_ref[...])
    ckv_ref[...] = ckv
    ckvt_ref[...] = jnp.transpose(ckv).astype(BF16)
    kr2 = _dot(xb, w_ref[:, o3:o3 + LANE]) * cs_ref[...]
    kr = kr2 + pltpu.roll(kr2, LANE // 2, axis=1)
    kr_ref[...] = kr[:, :rope]
    kcat_ref[:, :kvl] = ckv.astype(BF16)
    kcat_ref[:, kvl:] = kr.astype(BF16)


def _inproj(x, w_ext, l, gq, gkv, cs, *, ssm_w, ql, kvl, rope, tm, pre=None):
    split = isinstance(x, tuple)
    assert not (split and pre is not None)
    n = x[0].shape[0] + x[1].shape[0] if split else x.shape[0]
    d = w_ext.shape[1]
    cols = w_ext.shape[2]
    assert tm % LANE == 0
    ys, alpha = ([], 0.0) if pre is None else (list(pre[0]), pre[3])
    npt = x[0].shape[0] // tm if split else 0
    body = functools.partial(_inproj_body, ssm_w=ssm_w, ql=ql, kvl=kvl, rope=rope, n_y=len(ys), alpha=alpha,
                             npt=npt)
    if split:
        in_specs = list(_split_rows(tm, d, npt))
        args = list(x)
    else:
        in_specs = [_rows(tm, d)] * (1 + len(ys))
        args = [x, *ys]
    out_specs, out_shape = [], []
    if split:
        out_specs.append(_rows(tm, d))
        out_shape.append(jax.ShapeDtypeStruct((n, d), F32))
    if pre is not None:
        in_specs += [_layer(l - 1, (1, d)), _layer(l - 1, (1, d))]
        args += [pre[1], pre[2]]
        out_specs.append(_rows(tm, d))
        out_shape.append(jax.ShapeDtypeStruct((n, d), F32))
    in_specs += [_layer(l, (d, cols)), _layer(l, (1, ql)), _layer(l, (1, kvl)), _rows(tm, LANE)]
    args += [w_ext, gq, gkv, cs]
    out_specs += [_rows(tm, ssm_w), _rows(tm, ql), _rows(tm, kvl + LANE), _rows(tm, kvl),
                  _rows(tm, rope), pl.BlockSpec((kvl, tm), lambda i: (0, i))]
    out_shape += [jax.ShapeDtypeStruct((n, ssm_w), F32), jax.ShapeDtypeStruct((n, ql), BF16),
                  jax.ShapeDtypeStruct((n, kvl + LANE), BF16), jax.ShapeDtypeStruct((n, kvl), F32),
                  jax.ShapeDtypeStruct((n, rope), F32), jax.ShapeDtypeStruct((kvl, n), BF16)]
    return pl.pallas_call(
        body, grid=(n // tm,), in_specs=in_specs, out_specs=out_specs, out_shape=out_shape,
        compiler_params=_params("parallel"), name="in_proj")(*args)


def _qproj_body(cq_ref, wuq_ref, wukt_ref, cs_ref, q_ref, *, heads, nope, kvl, scale):
    q = _dot(cq_ref[...], wuq_ref[...])
    cs = cs_ref[...]
    lane = lax.broadcasted_iota(jnp.int32, cs.shape, 1)
    hw = nope + LANE
    qw = kvl + LANE
    for h in range(heads):
        qlat = _dot(q[:, h * hw:h * hw + nope].astype(BF16), wukt_ref[h])
        q2 = q[:, h * hw + nope:(h + 1) * hw] * cs
        qr = q2 + pltpu.roll(q2, LANE // 2, axis=1)
        qr = jnp.where(lane < LANE // 2, qr, 0.0)
        q_ref[:, h * qw:h * qw + kvl] = (qlat * scale).astype(BF16)
        q_ref[:, h * qw + kvl:(h + 1) * qw] = (qr * scale).astype(BF16)


def _qproj(cq, wuq_ext, wukt, l, cs, *, heads, nope, kvl, scale):
    n, ql = cq.shape
    hw = nope + LANE
    qw = kvl + LANE
    tm = _tile(n, 512)
    body = functools.partial(_qproj_body, heads=heads, nope=nope, kvl=kvl, scale=scale)
    return pl.pallas_call(
        body, grid=(n // tm,),
        in_specs=[_rows(tm, ql), _layer(l, (ql, heads * hw)), _layer(l, (heads, nope, kvl)),
                  _rows(tm, LANE)],
        out_specs=_rows(tm, heads * qw),
        out_shape=jax.ShapeDtypeStruct((n, heads * qw), BF16),
        compiler_params=_params("parallel"), name="q_proj")(cq, wuq_ext, wukt, cs)


def _flash_body(q_ref, k_ref, vt_ref, o_ref, m_ref, l_ref, acc_ref, *, heads, kvl, qw):
    i = pl.program_id(1)
    j = pl.program_id(2)

    @pl.when(j == 0)
    def _():
        m_ref[...] = jnp.full(m_ref.shape, NEG, F32)
        l_ref[...] = jnp.zeros(l_ref.shape, F32)
        acc_ref[...] = jnp.zeros(acc_ref.shape, F32)

    def step(masked):
        k = k_ref[...]
        vt = vt_ref[...]
        for h in range(heads):
            s = _dot_nt(k, q_ref[:, h * qw:(h + 1) * qw])
            if masked:
                key = lax.broadcasted_iota(jnp.int32, s.shape, 0)
                qry = lax.broadcasted_iota(jnp.int32, s.shape, 1)
                s = jnp.where(key <= qry, s, NEG)
            m_prev = m_ref[h]
            m_new = jnp.maximum(m_prev, jnp.max(s, axis=0, keepdims=True))
            a = jnp.exp(m_prev - m_new)
            p = jnp.exp(s - m_new)
            l_ref[h] = a * l_ref[h] + jnp.sum(p, axis=0, keepdims=True)
            acc_ref[h] = a * acc_ref[h] + _dot(vt, p.astype(BF16))
            m_ref[h] = m_new

    @pl.when(j < i)
    def _():
        step(False)

    @pl.when(j == i)
    def _():
        step(True)
        for h in range(heads):
            o_ref[:, h * kvl:(h + 1) * kvl] = jnp.transpose(acc_ref[h] / l_ref[h]).astype(BF16)


def _flash(qcat, kcat, ckv_t, *, batch, seq, heads, kvl):
    qw = kvl + LANE
    tq = _tile(seq, 512, mult=LANE)
    nq = seq // tq
    body = functools.partial(_flash_body, heads=heads, kvl=kvl, qw=qw)
    return pl.pallas_call(
        body, grid=(batch, nq, nq),
        in_specs=[pl.BlockSpec((tq, heads * qw), lambda b, i, j: (b * nq + i, 0)),
                  pl.BlockSpec((tq, qw), lambda b, i, j: (b * nq + jnp.minimum(i, j), 0)),
                  pl.BlockSpec((kvl, tq), lambda b, i, j: (0, b * nq + jnp.minimum(i, j)))],
        out_specs=pl.BlockSpec((tq, heads * kvl), lambda b, i, j: (b * nq + i, 0)),
        out_shape=jax.ShapeDtypeStruct((batch * seq, heads * kvl), BF16),
        scratch_shapes=[pltpu.VMEM((heads, 1, tq), F32), pltpu.VMEM((heads, 1, tq), F32),
                        pltpu.VMEM((heads, kvl, tq), F32)],
        compiler_params=_params("parallel", "parallel", "arbitrary"), name="prompt_attn")(
            qcat, kcat, ckv_t)


def _paged_body(pt_ref, q_ref, kn_ref, ckv_hbm, krt_hbm, o_ref, ckv_buf, krt_buf, sem, s_ref, kb_ref,
                *, layer, n_pages, page, kvl, rope, ts, chunk_pages):
    b = pl.program_id(0)
    nb = pl.num_programs(0)
    slot = lax.rem(b, 2)

    def fetch(bb, sl):
        def issue(i, c):
            for k in range(2):
                p = 2 * i + k
                pg = pt_ref[bb * n_pages + p]
                pltpu.make_async_copy(ckv_hbm.at[layer, pg], ckv_buf.at[sl, p], sem.at[sl, 0]).start(k)
                pltpu.make_async_copy(krt_hbm.at[layer, pg], krt_buf.at[sl, p], sem.at[sl, 1]).start(1 - k)
            return c
        lax.fori_loop(0, n_pages // 2, issue, 0, unroll=4)

    @pl.when(b == 0)
    def _():
        fetch(0, 0)

    @pl.when(b + 1 < nb)
    def _():
        fetch(b + 1, 1 - slot)

    pltpu.make_async_copy(ckv_buf.at[slot], ckv_buf.at[slot], sem.at[slot, 0]).wait()
    pltpu.make_async_copy(krt_buf.at[slot], krt_buf.at[slot], sem.at[slot, 1]).wait()

    q = q_ref[0]
    ql = q[:, :kvl]
    qr = q[:, kvl:kvl + rope]
    ck = chunk_pages * page
    n_chunks = n_pages // chunk_pages
    for c in range(n_chunks):
        kc = ckv_buf[slot, pl.ds(c * chunk_pages, chunk_pages)].reshape(ck, kvl).astype(BF16)
        rt = jnp.concatenate([krt_buf[slot, c * chunk_pages + p] for p in range(chunk_pages)],
                             axis=1).astype(BF16)
        s_ref[:, c * ck:(c + 1) * ck] = _dot_nt(ql, kc) + _dot(qr, rt)
        kb_ref[c * ck:(c + 1) * ck, :] = kc

    kn = kn_ref[0]
    sn = _dot_nt(q, kn)
    qj = lax.broadcasted_iota(jnp.int32, sn.shape, 0) & (ts - 1)
    col = lax.broadcasted_iota(jnp.int32, sn.shape, 1)
    sn = jnp.where(col <= qj, sn, NEG)

    m = jnp.maximum(jnp.max(s_ref[...], axis=1, keepdims=True), jnp.max(sn, axis=1, keepdims=True))
    pn = jnp.exp(sn - m)
    l = jnp.sum(pn, axis=1, keepdims=True)
    acc = _dot(pn.astype(BF16), kn[:, :kvl])
    for c in range(n_chunks):
        p = jnp.exp(s_ref[:, c * ck:(c + 1) * ck] - m)
        l = l + jnp.sum(p, axis=1, keepdims=True)
        acc = acc + _dot(p.astype(BF16), kb_ref[c * ck:(c + 1) * ck, :])
    o_ref[0] = (acc / l).astype(BF16)


def _paged(page_table, q_s, k_new, cache_ckv, cache_krope_t, l, *, ts, kvl, rope):
    bs, rows, qw = q_s.shape
    n_pages = page_table.shape[1]
    page = cache_ckv.shape[2]
    assert cache_krope_t.shape[2:] == (rope, page) and n_pages % 2 == 0
    chunk_pages = _tile(n_pages, 8, mult=1)
    past = n_pages * page
    assert ts & (ts - 1) == 0
    body = functools.partial(_paged_body, layer=l, n_pages=n_pages, page=page, kvl=kvl, rope=rope,
                             ts=ts, chunk_pages=chunk_pages)
    grid_spec = pltpu.PrefetchScalarGridSpec(
        num_scalar_prefetch=1, grid=(bs,),
        in_specs=[pl.BlockSpec((1, rows, qw), lambda b, pt: (b, 0, 0)),
                  pl.BlockSpec((1, k_new.shape[1], qw), lambda b, pt: (b, 0, 0)),
                  pl.BlockSpec(memory_space=pl.ANY), pl.BlockSpec(memory_space=pl.ANY)],
        out_specs=pl.BlockSpec((1, rows, kvl), lambda b, pt: (b, 0, 0)),
        scratch_shapes=[pltpu.VMEM((2, n_pages, page, kvl), F32),
                        pltpu.VMEM((2, n_pages, rope, page), F32),
                        pltpu.SemaphoreType.DMA((2, 2)),
                        pltpu.VMEM((rows, past), F32),
                        pltpu.VMEM((past, kvl), BF16)])
    return pl.pallas_call(
        body, grid_spec=grid_spec, out_shape=jax.ShapeDtypeStruct((bs, rows, kvl), BF16),
        compiler_params=_params("arbitrary"), name="paged_attn")(
            page_table.reshape(-1), q_s, k_new, cache_ckv, cache_krope_t)


def _s5_prep_body(lr_ref, li_ref, ldt_ref, br_ref, bi_ref, ar_ref, ai_ref, bbr_ref, bbi_ref):
    lr = lr_ref[...]
    li = li_ref[...]
    dt = jnp.exp(ldt_ref[...])
    mag = jnp.exp(lr * dt)
    ar = mag * jnp.cos(li * dt)
    ai = mag * jnp.sin(li * dt)
    den = lr * lr + li * li
    zr = ((ar - 1.0) * lr + ai * li) / den
    zi = (ai * lr - (ar - 1.0) * li) / den
    ar_ref[...] = ar
    ai_ref[...] = ai
    br = br_ref[...]
    bi = bi_ref[...]
    bbr_ref[...] = zr * br - zi * bi
    bbi_ref[...] = zr * bi + zi * br


def _s5_prep(lam_re, lam_im, log_dt, b_re_t, b_im_t):
    nl, g, _, p = lam_re.shape
    c = b_re_t.shape[2]
    gp = pl.BlockSpec((None, g, 1, p), lambda l: (l, 0, 0, 0))
    gcp = pl.BlockSpec((None, g, c, p), lambda l: (l, 0, 0, 0))
    return pl.pallas_call(
        _s5_prep_body, grid=(nl,),
        in_specs=[gp, gp, pl.BlockSpec((None, g, 1, 1), lambda l: (l, 0, 0, 0)), gcp, gcp],
        out_specs=[gp, gp, gcp, gcp],
        out_shape=[jax.ShapeDtypeStruct((nl, g, 1, p), F32)] * 2
        + [jax.ShapeDtypeStruct((nl, g, c, p), F32)] * 2,
        compiler_params=_params("parallel"), name="s5_prep")(lam_re, lam_im, log_dt, b_re_t, b_im_t)


def _swap_halves(z, nb):
    if 2 * nb == SUBLANE:
        return pltpu.roll(z, nb, axis=0)
    return jnp.concatenate([z[nb:], z[:nb]], axis=0)


def _s5_scan_body(u_ref, bw_ref, cw_ref, ar_ref, ai_ref, s0_ref, y_ref, f_ref, x_ref, st_ref,
                  *, nb, tc, nslab, sc, sp, lw):
    step_rows = 2 * nb
    w = nslab * sc

    @pl.when(pl.program_id(0) == 0)
    def _():
        st_ref[...] = s0_ref[...]

    for j in range(nslab):
        lhs = jnp.concatenate([u_ref[:, j * sc:(j + 1) * sc], u_ref[:, w + j * sc:w + (j + 1) * sc]], axis=1)
        x_ref[:, j * sp:(j + 1) * sp] = _dot(lhs, bw_ref[j])

    im_row = lax.broadcasted_iota(jnp.int32, (step_rows, lw), 0) >= nb
    for p0 in range(0, nslab * sp, lw):
        lanes = slice(p0, p0 + lw)
        a1 = jnp.broadcast_to(ar_ref[:, lanes], (step_rows, lw))
        ai = jnp.broadcast_to(ai_ref[:, lanes], (step_rows, lw))
        a2 = jnp.where(im_row, ai, -ai)
        z = st_ref[:, lanes]
        for t in range(tc):
            rows = slice(t * step_rows, (t + 1) * step_rows)
            z = a1 * z + a2 * _swap_halves(z, nb) + x_ref[rows, lanes]
            x_ref[rows, lanes] = z
        st_ref[:, lanes] = z

    re_row = (lax.broadcasted_iota(jnp.int32, (tc * step_rows, sc), 0) & nb) == 0
    for j in range(nslab):
        y2 = _dot(x_ref[:, j * sp:(j + 1) * sp].astype(BF16), cw_ref[j])
        y_ref[:, j * sc:(j + 1) * sc] = jnp.where(re_row, y2[:, :sc], y2[:, sc:])
    f_ref[...] = st_ref[...]


def _s5_scan(u2, nb, blk, l, s0):
    bw, cw, ar, ai = blk
    rows, w2 = u2.shape
    nslab, sc2, sp = bw.shape[1:]
    sc = sc2 // 2
    w = w2 // 2
    gp = nslab * sp
    step_rows = 2 * nb
    assert nb & (nb - 1) == 0 and step_rows % SUBLANE == 0
    steps = rows // step_rows
    tc = _tile(steps, max(1, 256 // step_rows), mult=1)
    rc = tc * step_rows
    lw = _tile(gp, max(LANE, (8 * SUBLANE * LANE) // step_rows), mult=LANE)
    body = functools.partial(_s5_scan_body, nb=nb, tc=tc, nslab=nslab, sc=sc, sp=sp, lw=lw)
    return pl.pallas_call(
        body, grid=(steps // tc,),
        in_specs=[_rows(rc, w2), _layer(l, (nslab, sc2, sp)), _layer(l, (nslab, sp, sc2)),
                  _layer(l, (1, gp)), _layer(l, (1, gp)), _const((step_rows, gp))],
        out_specs=[_rows(rc, w), _const((step_rows, gp))],
        out_shape=[jax.ShapeDtypeStruct((rows, w), F32), jax.ShapeDtypeStruct((step_rows, gp), F32)],
        scratch_shapes=[pltpu.VMEM((rc, gp), F32), pltpu.VMEM((step_rows, gp), F32)],
        compiler_params=_params("arbitrary"), name="s5_scan")(u2, bw, cw, ar, ai, s0)


def _s5_scan_tok_body(*refs, nreal, nb, tc, pitch, nslab, sc, sp, lw):
    u_refs = refs[:nreal]
    (bre_ref, bim_ref, cre_ref, cim_ref, a1_ref, a2_ref, y_ref, f_ref, ub_ref, x_ref, st_ref) = refs[nreal:]
    step_rows = 2 * nb
    half = nb * pitch
    nq = sp // LANE

    @pl.when(pl.program_id(0) == 0)
    def _():
        st_ref[...] = jnp.zeros(st_ref.shape, F32)
        ub_ref[...] = jnp.zeros(ub_ref.shape, BF16)

    for b in range(nreal):
        ub_ref[b * pitch:b * pitch + tc, :] = u_refs[b][...].astype(BF16)
    for j in range(nslab):
        uj = ub_ref[:, j * sc:(j + 1) * sc]
        xr = _dot(uj, bre_ref[j])
        xi = _dot(uj, bim_ref[j])
        for q in range(nq):
            x_ref[j * nq + q, 0:half, :] = xr[:, q * LANE:(q + 1) * LANE]
            x_ref[j * nq + q, half:2 * half, :] = xi[:, q * LANE:(q + 1) * LANE]

    npass = nslab * sp // lw
    nl = lw // LANE
    zs = [st_ref[:, p * lw:(p + 1) * lw] for p in range(npass)]
    for t in range(tc):
        rows = pl.ds(t, step_rows, stride=pitch)
        for p in range(npass):
            lanes = slice(p * lw, (p + 1) * lw)
            x_t = jnp.concatenate([x_ref[p * nl + q, rows, :] for q in range(nl)], axis=1)
            z = a1_ref[:, lanes] * zs[p] + a2_ref[:, lanes] * _swap_halves(zs[p], nb) + x_t
            for q in range(nl):
                x_ref[p * nl + q, rows, :] = z[:, q * LANE:(q + 1) * LANE]
            zs[p] = z
    for p in range(npass):
        st_ref[:, p * lw:(p + 1) * lw] = zs[p]

    for j in range(nslab):
        sr = jnp.concatenate([x_ref[j * nq + q, 0:half, :] for q in range(nq)], axis=1).astype(BF16)
        si = jnp.concatenate([x_ref[j * nq + q, half:2 * half, :] for q in range(nq)], axis=1).astype(BF16)
        y = _dot(sr, cre_ref[j]) - _dot(si, cim_ref[j])
        for b in range(nreal):
            y_ref[b, :, j * sc:(j + 1) * sc] = y[b * pitch:b * pitch + tc]
    f_ref[...] = st_ref[...]


def _s5_scan_tok(u, blk, l, *, batch, seq, nb):
    bre, bim, cre, cim, a1, a2 = blk
    w = u.shape[1]
    nslab, sc, sp = bre.shape[1:]
    gp = nslab * sp
    step_rows = 2 * nb
    assert step_rows == SUBLANE and batch <= nb
    tc = _tile(seq, 64, mult=2 * SUBLANE)
    pitch = tc + SUBLANE
    nt = seq // tc
    lw = _tile(gp, 8 * LANE, mult=LANE)
    body = functools.partial(_s5_scan_tok_body, nreal=batch, nb=nb, tc=tc, pitch=pitch, nslab=nslab,
                             sc=sc, sp=sp, lw=lw)
    u_specs = [pl.BlockSpec((tc, w), functools.partial(lambda i, b: (b * nt + i, 0), b=b))
               for b in range(batch)]
    return pl.pallas_call(
        body, grid=(nt,),
        in_specs=u_specs + [_layer(l, (nslab, sc, sp)), _layer(l, (nslab, sc, sp)),
                            _layer(l, (nslab, sp, sc)), _layer(l, (nslab, sp, sc)),
                            _layer(l, (step_rows, gp)), _layer(l, (step_rows, gp))],
        out_specs=[pl.BlockSpec((batch, tc, w), lambda i: (0, i, 0)), _const((step_rows, gp))],
        out_shape=[jax.ShapeDtypeStruct((batch, seq, w), F32), jax.ShapeDtypeStruct((step_rows, gp), F32)],
        scratch_shapes=[pltpu.VMEM((nb * pitch, w), BF16), pltpu.VMEM((gp // LANE, step_rows * pitch, LANE), F32),
                        pltpu.VMEM((step_rows, gp), F32)],
        compiler_params=_params("arbitrary"), name="s5_scan_tok")(
            *([u] * batch), bre, bim, cre, cim, a1, a2)


def _s5_post_body(yp_ref, sre_ref, sim_ref, u_ref, d_ref, wglu_ref, g_ref, o_ref, *, npt):
    y = jnp.where(pl.program_id(0) < npt, yp_ref[...], sre_ref[...] + sim_ref[...])
    yf = y + d_ref[...] * u_ref[...]
    g = jax.nn.gelu(yf)
    z = g * jax.nn.sigmoid(_dot(g.astype(BF16), wglu_ref[...]))
    o_ref[...] = _rms(z, g_ref[...]).astype(BF16)


def _s5_post(y_p, y_s, u, d, wglu, g, l, *, tm):
    n, w = u.shape
    npt = y_p.shape[0] // tm

    def smap(part):
        return pl.BlockSpec((None, tm, w), lambda i: (part, jnp.maximum(i - npt, 0), 0))

    return pl.pallas_call(
        functools.partial(_s5_post_body, npt=npt), grid=(n // tm,),
        in_specs=[_split_rows(tm, w, npt)[0], smap(0), smap(1), _rows(tm, w), _layer(l, (1, w)),
                  _layer(l, (w, w)), _layer(l, (1, w))],
        out_specs=_rows(tm, w), out_shape=jax.ShapeDtypeStruct((n, w), BF16),
        compiler_params=_params("parallel"), name="s5_post")(y_p, y_s, y_s, u, d, wglu, g)


def _split_rows(tm, w, npt):
    return (pl.BlockSpec((tm, w), lambda i: (jnp.minimum(i, npt - 1), 0)),
            pl.BlockSpec((tm, w), lambda i: (jnp.maximum(i - npt, 0), 0)))


def _mixout_body(op_ref, os_ref, ssm_ref, x_ref, wuv_ref, gm_ref, wout_ref, g1_ref, b1_ref, wmq_ref,
                 x1_ref, qm_ref, *, heads, kvl, ssm_w, alpha, mscale, npt):
    is_prompt = pl.program_id(0) < npt

    def o_head(h):
        cols = slice(h * kvl, (h + 1) * kvl)
        return jnp.where(is_prompt, op_ref[:, cols], os_ref[:, cols])

    mla = jnp.concatenate([_dot(o_head(h), wuv_ref[h]) for h in range(heads)], axis=-1)
    mla_n = _rms(mla, gm_ref[...]).astype(BF16)
    m = _dot(ssm_ref[...], wout_ref[:ssm_w, :]) + _dot(mla_n, wout_ref[ssm_w:, :])
    x1 = _ln(alpha * x_ref[...] + m, g1_ref[...], b1_ref[...])
    x1_ref[...] = x1
    qm_ref[...] = (_dot(x1.astype(BF16), wmq_ref[...]) * mscale).astype(BF16)


def _mixout(o_p, o_s, ssm_n, x, wuv, gm, wout, g1, b1, wmq, l, *, heads, kvl, alpha, mscale, tm):
    n, d = x.shape
    ssm_w = ssm_n.shape[1]
    v = wuv.shape[3]
    mix = wout.shape[1]
    mq = wmq.shape[2]
    npt = o_p.shape[0] // tm
    body = functools.partial(_mixout_body, heads=heads, kvl=kvl, ssm_w=ssm_w, alpha=alpha, mscale=mscale,
                             npt=npt)
    return pl.pallas_call(
        body, grid=(n // tm,),
        in_specs=[*_split_rows(tm, heads * kvl, npt), _rows(tm, ssm_w), _rows(tm, d),
                  _layer(l, (heads, kvl, v)), _layer(l, (1, heads * v)), _layer(l, (mix, d)),
                  _layer(l, (1, d)), _layer(l, (1, d)), _layer(l, (d, mq))],
        out_specs=[_rows(tm, d), _rows(tm, mq)],
        out_shape=[jax.ShapeDtypeStruct((n, d), F32), jax.ShapeDtypeStruct((n, mq), BF16)],
        compiler_params=_params("parallel"), name="mix_out")(
            o_p, o_s, ssm_n, x, wuv, gm, wout, g1, b1, wmq)


def _memkv_body(mem_ref, wk_ref, wv_ref, k_ref, v_ref):
    mb = mem_ref[...].astype(BF16)
    k_ref[...] = _dot(mb, wk_ref[...])
    v_ref[...] = _dot(mb, wv_ref[...])


def _memkv(mem, wk, wv):
    nl, d, hk = wk.shape
    rows = mem.shape[0]
    w = pl.BlockSpec((None, d, hk), lambda l: (l, 0, 0))
    o = pl.BlockSpec((None, rows, hk), lambda l: (l, 0, 0))
    return pl.pallas_call(
        _memkv_body, grid=(nl,), in_specs=[_const((rows, d)), w, w], out_specs=[o, o],
        out_shape=[jax.ShapeDtypeStruct((nl, rows, hk), F32)] * 2,
        compiler_params=_params("parallel"), name="mem_kv")(mem, wk, wv)


def _softmax_rows(s):
    p = jnp.exp(s - jnp.max(s, axis=1, keepdims=True))
    return p / jnp.sum(p, axis=1, keepdims=True)


def _cross_p_body(q_ref, mk_ref, mv_ref, o_ref, *, mh, md):
    mk = mk_ref[...].astype(BF16)
    mv = mv_ref[...].astype(BF16)
    for h in range(mh):
        cols = slice(h * md, (h + 1) * md)
        p = _softmax_rows(_dot_nt(q_ref[:, cols], mk[:, cols]))
        o_ref[:, cols] = _dot(p.astype(BF16), mv[:, cols]).astype(BF16)


def _cross_prompt(qm, memk, memv, l, *, batch, seq, mh, md):
    mlen = memk.shape[1] // batch
    hk = mh * md
    tm = _tile(seq, 512)
    nt = seq // tm
    kv = pl.BlockSpec((None, mlen, hk), lambda b, i: (l, b, 0))
    body = functools.partial(_cross_p_body, mh=mh, md=md)
    return pl.pallas_call(
        body, grid=(batch, nt),
        in_specs=[pl.BlockSpec((tm, hk), lambda b, i: (b * nt + i, 0)), kv, kv],
        out_specs=pl.BlockSpec((tm, hk), lambda b, i: (b * nt + i, 0)),
        out_shape=jax.ShapeDtypeStruct((batch * seq, hk), BF16),
        compiler_params=_params("parallel", "parallel"), name="cross_prompt")(qm, memk, memv)


def _cross_s_body(q_ref, mk_ref, mv_ref, o_ref, *, bb, ts, mh, md):
    q_all = q_ref[...].astype(F32)
    rows_kv = mk_ref.shape[1]
    row_head = lax.broadcasted_iota(jnp.int32, (mh * ts, rows_kv), 0) >> (ts.bit_length() - 1)
    col_head = lax.broadcasted_iota(jnp.int32, (mh * ts, rows_kv), 1) & (mh - 1)
    own = row_head == col_head
    outs = []
    for b in range(bb):
        q = q_all[b * ts:(b + 1) * ts, :]
        qh = jnp.concatenate([q[:, h * md:(h + 1) * md] for h in range(mh)], axis=0).astype(BF16)
        s = jnp.where(own, _dot_nt(qh, mk_ref[b].astype(BF16)), NEG)
        o = _dot(_softmax_rows(s).astype(BF16), mv_ref[b].astype(BF16))
        outs.append(jnp.concatenate([o[h * ts:(h + 1) * ts] for h in range(mh)], axis=1))
    o_ref[...] = jnp.concatenate(outs, axis=0).astype(BF16)


def _cross_sample(qm_s, cache_k, cache_v, l, *, bs, ts, mh, md):
    rows_kv = cache_k.shape[2]
    hk = mh * md
    assert mh & (mh - 1) == 0 and ts & (ts - 1) == 0
    bb = _tile(bs, 8, mult=1)
    kv = pl.BlockSpec((None, bb, rows_kv, md), lambda i: (l, i, 0, 0))
    body = functools.partial(_cross_s_body, bb=bb, ts=ts, mh=mh, md=md)
    return pl.pallas_call(
        body, grid=(bs // bb,),
        in_specs=[_rows(bb * ts, hk), kv, kv], out_specs=_rows(bb * ts, hk),
        out_shape=jax.ShapeDtypeStruct((bs * ts, hk), BF16),
        compiler_params=_params("parallel"), name="cross_sample")(qm_s, cache_k, cache_v)


def _top2(lt, n_exp):
    e = jnp.exp(lt - jnp.max(lt, axis=0, keepdims=True))
    probs = e / jnp.sum(e, axis=0, keepdims=True)
    eidx = lax.broadcasted_iota(jnp.int32, probs.shape, 0)
    m1 = jnp.max(probs, axis=0, keepdims=True)
    i1 = jnp.min(jnp.where(probs == m1, eidx, n_exp), axis=0, keepdims=True)
    rest = jnp.where(eidx == i1, -1.0, probs)
    m2 = jnp.max(rest, axis=0, keepdims=True)
    i2 = jnp.min(jnp.where(rest == m2, eidx, n_exp), axis=0, keepdims=True)
    den = m1 + m2
    return i1, i2, m1 / den, m2 / den


def _crossout_body(*refs, alpha, n_exp, npt):
    if n_exp:
        op_ref, os_ref, x1_ref, wmo_ref, g_ref, b_ref, wrh_ref, wrl_ref, x2_ref, ti_ref, tg_ref = refs
    else:
        op_ref, os_ref, x1_ref, wmo_ref, g_ref, b_ref, x2_ref = refs
    o = jnp.where(pl.program_id(0) < npt, op_ref[...], os_ref[...])
    x2 = _ln(alpha * x1_ref[...] + _dot(o, wmo_ref[...]), g_ref[...], b_ref[...])
    x2_ref[...] = x2
    if n_exp:
        xh = x2.astype(BF16)
        xl = (x2 - xh.astype(F32)).astype(BF16)
        wh = wrh_ref[...]
        lt = _dot_nt(wh, xh) + (_dot_nt(wh, xl) + _dot_nt(wrl_ref[...], xh))
        i1, i2, g1, g2 = _top2(lt[:n_exp], n_exp)
        ti_ref[0:1, :] = i1
        ti_ref[1:2, :] = i2
        tg_ref[0:1, :] = g1
        tg_ref[1:2, :] = g2


def _crossout(o_p, o_s, x1, wmo, g, b, l, *, alpha, tm, router=None):
    n, d = x1.shape
    hk = o_p.shape[1]
    assert tm % LANE == 0
    npt = o_p.shape[0] // tm
    in_specs = [*_split_rows(tm, hk, npt), _rows(tm, d), _layer(l, (hk, d)), _layer(l, (1, d)),
                _layer(l, (1, d))]
    out_specs = [_rows(tm, d)]
    out_shape = [jax.ShapeDtypeStruct((n, d), F32)]
    args = [o_p, o_s, x1, wmo, g, b]
    n_exp = 0
    if router is not None:
        wrh, wrl, li, n_exp = router
        er = wrh.shape[1]
        in_specs += [_layer(li, (er, d)), _layer(li, (er, d))]
        top = pl.BlockSpec((2, tm), lambda i: (0, i))
        out_specs += [top, top]
        out_shape += [jax.ShapeDtypeStruct((2, n), jnp.int32), jax.ShapeDtypeStruct((2, n), F32)]
        args += [wrh, wrl]
    body = functools.partial(_crossout_body, alpha=alpha, n_exp=n_exp, npt=npt)
    return pl.pallas_call(
        body, grid=(n // tm,), in_specs=in_specs, out_specs=out_specs, out_shape=out_shape,
        compiler_params=_params("parallel"), name="cross_out")(*args)


def _ffn_body(te_ref, tv_ref, *refs, scaled):
    if scaled:
        x_ref, wg_ref, wu_ref, wd_ref, sc_ref, o_ref, xb_ref = refs
    else:
        x_ref, wg_ref, wu_ref, wd_ref, o_ref, xb_ref = refs
    i = pl.program_id(0)
    j = pl.program_id(1)
    last = pl.num_programs(1) - 1

    @pl.when(tv_ref[i] > 0)
    def _():
        @pl.when(j == 0)
        def _():
            xb_ref[...] = x_ref[...].astype(BF16)

        x = xb_ref[...]
        h = (jax.nn.silu(_dot(x, wg_ref[...])) * _dot(x, wu_ref[...])).astype(BF16)
        y = _dot(h, wd_ref[...])

        @pl.when(j == 0)
        def _():
            o_ref[...] = y

        @pl.when(j > 0)
        def _():
            o_ref[...] += y

        if scaled:
            @pl.when(j == last)
            def _():
                o_ref[...] *= sc_ref[...]

    @pl.when(tv_ref[i] == 0)
    def _():
        o_ref[...] = jnp.zeros(o_ref.shape, F32)


def _ffn(xs, wg, wu, wd, tile_expert, tile_valid, tm, tf, scale=None):
    rows, d = xs.shape
    ff = wg.shape[2]
    in_specs = [pl.BlockSpec((tm, d), lambda i, j, te, tv: (i, 0)),
                pl.BlockSpec((None, d, tf), lambda i, j, te, tv: (te[i], 0, j)),
                pl.BlockSpec((None, d, tf), lambda i, j, te, tv: (te[i], 0, j)),
                pl.BlockSpec((None, tf, d), lambda i, j, te, tv: (te[i], j, 0))]
    args = [xs, wg, wu, wd]
    if scale is not None:
        in_specs.append(pl.BlockSpec((tm, 1), lambda i, j, te, tv: (i, 0)))
        args.append(scale)
    grid_spec = pltpu.PrefetchScalarGridSpec(
        num_scalar_prefetch=2, grid=(rows // tm, ff // tf), in_specs=in_specs,
        out_specs=pl.BlockSpec((tm, d), lambda i, j, te, tv: (i, 0)),
        scratch_shapes=[pltpu.VMEM((tm, d), BF16)])
    return pl.pallas_call(
        functools.partial(_ffn_body, scaled=scale is not None), grid_spec=grid_spec,
        out_shape=jax.ShapeDtypeStruct((rows, d), F32),
        compiler_params=_params("parallel", "arbitrary"), name="ffn")(tile_expert, tile_valid, *args)


def _addln_body(*refs, alpha, n_y, npt):
    x_ref = refs[0]
    y_refs = refs[1:1 + n_y]
    g_ref, b_ref, op_ref, os_ref = refs[1 + n_y:]
    y = y_refs[0][...]
    for r in y_refs[1:]:
        y = y + r[...]
    out = _ln(alpha * x_ref[...] + y, g_ref[...], b_ref[...])

    @pl.when(pl.program_id(0) < npt)
    def _():
        op_ref[...] = out

    @pl.when(pl.program_id(0) >= npt)
    def _():
        os_ref[...] = out


def _addln(x, ys, g, b, l, *, alpha, tm, n_p):
    n, d = x.shape
    npt = n_p // tm
    body = functools.partial(_addln_body, alpha=alpha, n_y=len(ys), npt=npt)
    return pl.pallas_call(
        body, grid=(n // tm,),
        in_specs=[_rows(tm, d)] * (1 + len(ys)) + [_layer(l, (1, d)), _layer(l, (1, d))],
        out_specs=list(_split_rows(tm, d, npt)),
        out_shape=[jax.ShapeDtypeStruct((n_p, d), F32), jax.ShapeDtypeStruct((n - n_p, d), F32)],
        compiler_params=_params("arbitrary"), name="add_ln")(x, *ys, g, b)


def _cast_cols_body(w_ref, o_ref, *, f):
    o_ref[:, :f] = w_ref[...].astype(BF16)
    if o_ref.shape[1] > f:
        o_ref[:, f:] = jnp.zeros((o_ref.shape[0], o_ref.shape[1] - f), BF16)


def _cast_pad_cols(w, fp):
    e, d, f = w.shape
    td = _tile(d, 256)
    return pl.pallas_call(
        functools.partial(_cast_cols_body, f=f), grid=(e, d // td),
        in_specs=[pl.BlockSpec((None, td, f), lambda a, b: (a, b, 0))],
        out_specs=pl.BlockSpec((None, td, fp), lambda a, b: (a, b, 0)),
        out_shape=jax.ShapeDtypeStruct((e, d, fp), BF16),
        compiler_params=_params("parallel", "parallel"), name="cast_cols")(w)


def _cast_rows_body(w_ref, o_ref, *, nvalid):
    j = pl.program_id(1)

    @pl.when(j < nvalid)
    def _():
        o_ref[...] = w_ref[...].astype(BF16)

    @pl.when(j >= nvalid)
    def _():
        o_ref[...] = jnp.zeros(o_ref.shape, BF16)


def _cast_pad_rows(w, fp):
    e, f, d = w.shape
    tr = _tile(math.gcd(f, fp), 512, mult=2 * SUBLANE)
    nvalid = f // tr
    return pl.pallas_call(
        functools.partial(_cast_rows_body, nvalid=nvalid), grid=(e, fp // tr),
        in_specs=[pl.BlockSpec((None, tr, d), lambda a, b: (a, jnp.minimum(b, nvalid - 1), 0))],
        out_specs=pl.BlockSpec((None, tr, d), lambda a, b: (a, b, 0)),
        out_shape=jax.ShapeDtypeStruct((e, fp, d), BF16),
        compiler_params=_params("parallel", "parallel"), name="cast_rows")(w)


def _moe_plan(top_i, top_g, n_exp, tm):
    n = top_i.shape[1]
    flat_e = top_i.reshape(-1)
    onehot = (flat_e[:, None] == jnp.arange(n_exp, dtype=jnp.int32)[None, :]).astype(jnp.int32)
    csum = jnp.cumsum(onehot, axis=0)
    rank = jnp.sum(onehot * csum, axis=1) - 1
    counts = csum[-1]
    padded = ((counts + tm - 1) // tm) * tm
    ends = jnp.cumsum(padded)
    starts = ends - padded
    cstart = jnp.cumsum(counts) - counts
    pos = jnp.sum(onehot * starts[None, :], axis=1) + rank
    n_tiles = -(-2 * n // tm) + n_exp
    n_slots = n_tiles * tm
    order = jnp.argsort(flat_e, stable=True).astype(jnp.int32)
    tile_start = jnp.arange(n_tiles, dtype=jnp.int32) * tm
    tile_e = jnp.sum((tile_start[:, None] >= ends[None, :]).astype(jnp.int32), axis=1)
    tile_valid = (tile_e < n_exp).astype(jnp.int32)
    tile_e = jnp.minimum(tile_e, n_exp - 1)
    slot = jnp.arange(n_slots, dtype=jnp.int32)
    slot_e = jnp.repeat(tile_e, tm)
    eh = (slot_e[:, None] == jnp.arange(n_exp, dtype=jnp.int32)[None, :]).astype(jnp.int32)
    within = slot - jnp.sum(eh * starts[None, :], axis=1)
    live = (within < jnp.sum(eh * counts[None, :], axis=1)) & (jnp.repeat(tile_valid, tm) > 0)
    src = jnp.where(live, within + jnp.sum(eh * cstart[None, :], axis=1), slot % (2 * n))
    src_flat = jnp.take(order, src)
    src_token = src_flat % n
    slot_gate = jnp.where(live, jnp.take(top_g.reshape(-1), src_flat), 0.0)
    return src_token, slot_gate[:, None], tile_e, tile_valid, pos[:n], pos[n:]


def kernel(x_prompt, x_sample, mem_prompt, cache_ckv, cache_krope, cache_mem_k, cache_mem_v, state_ssm_re, state_ssm_im, page_table, w_in, g_q, w_uq, g_kv, w_uk, w_uv, ssm_lambda_re, ssm_lambda_im, ssm_log_dt, ssm_b_re, ssm_b_im, ssm_c_re, ssm_c_im, ssm_d, w_glu, g_ssm_out, g_mla_out, w_out, ln1_g, ln1_b, w_mq, w_mk, w_mv, w_mo, ln2_g, ln2_b, w_ff_gate, w_ff_up, w_ff_down, w_router, w_e_gate, w_e_up, w_e_down, ln3_g, ln3_b):
    bp_, tp, d = x_prompt.shape
    bs, ts, _ = x_sample.shape
    depth = w_in.shape[0]
    ssm_w = ssm_d.shape[1]
    groups, pstate = ssm_lambda_re.shape[1:]
    gch = ssm_b_re.shape[-1]
    ql = g_q.shape[1]
    kvl = g_kv.shape[1]
    heads = w_uq.shape[2]
    nope = w_uk.shape[3]
    rope = w_uq.shape[3] - nope
    vdim = w_uv.shape[3]
    mlen = mem_prompt.shape[1]
    mh, md = w_mq.shape[2:]
    n_exp = w_router.shape[2]
    n_pages, page = page_table.shape[1], cache_ckv.shape[2]
    past = n_pages * page
    n_p, n_s = bp_ * tp, bs * ts
    n = n_p + n_s
    assert 2 * rope == LANE and groups % SLAB_GROUPS == 0
    assert SLAB_GROUPS * gch == LANE and bp_ <= SUBLANE // 2
    alpha = (2.0 * depth) ** 0.25
    att_scale = (nope + rope) ** -0.5
    mem_scale = md ** -0.5
    qw = kvl + LANE
    half = rope // 2

    o3 = ssm_w + ql + kvl
    w_in_ext = jnp.concatenate([w_in, w_in[:, :, o3 + half:o3 + rope], w_in[:, :, o3:o3 + half]],
                               axis=2).astype(BF16)
    uq_n, uq_r = w_uq[..., :nope], w_uq[..., nope:]
    w_uq_ext = jnp.concatenate([uq_n, uq_r, uq_r[..., half:], uq_r[..., :half]], axis=-1)
    w_uq_ext = w_uq_ext.reshape(depth, ql, heads * (nope + LANE)).astype(BF16)
    w_ukt = jnp.transpose(w_uk, (0, 2, 3, 1)).astype(BF16)
    w_uv_h = jnp.transpose(w_uv, (0, 2, 1, 3)).astype(BF16)
    w_glu_b = w_glu.astype(BF16)
    w_out_b = w_out.astype(BF16)
    w_mq_b = w_mq.reshape(depth, d, mh * md).astype(BF16)
    w_mk_b = w_mk.reshape(depth, d, mh * md).astype(BF16)
    w_mv_b = w_mv.reshape(depth, d, mh * md).astype(BF16)
    w_mo_b = w_mo.reshape(depth, mh * md, d).astype(BF16)
    row = lambda a: a[:, None, :]
    g_q_r, g_kv_r, d_r = row(g_q), row(g_kv), row(ssm_d)
    g_ssm_r, g_mla_r = row(g_ssm_out), row(g_mla_out)
    ln1g, ln1b, ln2g, ln2b, ln3g, ln3b = (row(a) for a in (ln1_g, ln1_b, ln2_g, ln2_b, ln3_g, ln3_b))

    tf_dense = _tile(w_ff_gate.shape[2], 512, mult=LANE)
    dff = w_ff_gate.shape[2]
    ffg, ffu, ffd = _cast_pad_cols(w_ff_gate, dff), _cast_pad_cols(w_ff_up, dff), _cast_pad_rows(w_ff_down, dff)
    eff = w_e_gate.shape[3]
    tf_moe = min(1024, -(-eff // LANE) * LANE)
    effp = -(-eff // tf_moe) * tf_moe
    n_moe = w_e_gate.shape[0]
    eg = _cast_pad_cols(w_e_gate.reshape(n_moe * n_exp, d, eff), effp)
    eu = _cast_pad_cols(w_e_up.reshape(n_moe * n_exp, d, eff), effp)
    ed = _cast_pad_rows(w_e_down.reshape(n_moe * n_exp, eff, d), effp)
    er = -(-n_exp // 16) * 16
    wr_t = jnp.pad(jnp.transpose(w_router, (0, 2, 1)), ((0, 0), (0, er - n_exp), (0, 0)))
    wr_hi = wr_t.astype(BF16)
    wr_lo = (wr_t - wr_hi.astype(F32)).astype(BF16)

    a_re, a_im, bb_re, bb_im = _s5_prep(ssm_lambda_re[:, :, None, :], ssm_lambda_im[:, :, None, :],
                                        ssm_log_dt[:, :, None, None],
                                        jnp.swapaxes(ssm_b_re, 2, 3), jnp.swapaxes(ssm_b_im, 2, 3))
    nslab = groups // SLAB_GROUPS
    eye = jnp.eye(SLAB_GROUPS, dtype=F32)

    def b_blocks(bb):
        bb = bb.reshape(depth, nslab, SLAB_GROUPS, gch, pstate)
        return jnp.einsum('lsgcp,gh->lsgchp', bb, eye).reshape(
            depth, nslab, SLAB_GROUPS * gch, SLAB_GROUPS * pstate).astype(BF16)

    def c_blocks(c):
        c = c.reshape(depth, nslab, SLAB_GROUPS, gch, pstate)
        return jnp.einsum('lsgcp,gh->lshpgc', c, eye).reshape(
            depth, nslab, SLAB_GROUPS * pstate, SLAB_GROUPS * gch).astype(BF16)

    bre_blk, bim_blk = b_blocks(bb_re), b_blocks(bb_im)
    cre_blk, cim_blk = c_blocks(ssm_c_re), c_blocks(ssm_c_im)
    a_re = a_re.reshape(depth, 1, groups * pstate)
    a_im = a_im.reshape(depth, 1, groups * pstate)
    s5_blk = (jnp.concatenate([bre_blk, bim_blk], axis=2), jnp.concatenate([cre_blk, -cim_blk], axis=3),
              a_re, a_im)
    nbp = SUBLANE // 2
    a1_rows = jnp.broadcast_to(a_re, (depth, 2 * nbp, groups * pstate))
    a2_rows = jnp.concatenate([jnp.broadcast_to(-a_im, (depth, nbp, groups * pstate)),
                               jnp.broadcast_to(a_im, (depth, nbp, groups * pstate))], axis=1)
    s5_tok_blk = (bre_blk, bim_blk, cre_blk, cim_blk, a1_rows, a2_rows)

    pos = jnp.concatenate([jnp.tile(jnp.arange(tp, dtype=F32), bp_),
                           jnp.tile(past + jnp.arange(ts, dtype=F32), bs)])[:, None]
    cs = _rope_table(pos, rope)
    memk, memv = _memkv(mem_prompt.reshape(bp_ * mlen, d), w_mk_b, w_mv_b)
    cache_k = cache_mem_k.reshape(depth, bs, mlen * mh, md)
    cache_v = cache_mem_v.reshape(depth, bs, mlen * mh, md)
    cache_krope_t = jnp.swapaxes(cache_krope, 2, 3)
    kn_rows = LANE
    tm_tok = _tile(math.gcd(tp, n_s), 512)

    def scan_rows(u_bt):
        u_t = jnp.swapaxes(u_bt, 0, 1).astype(BF16)
        zz = jnp.zeros_like(u_t)
        rows = jnp.stack([jnp.concatenate([u_t, zz], -1), jnp.concatenate([zz, u_t], -1)], axis=1)
        return rows.reshape(-1, 2 * ssm_w)

    tm_ffn = _tile(n, 512)
    dense_te = lambda i: jnp.full((n // tm_ffn,), i, jnp.int32)
    dense_tv = jnp.ones((n // tm_ffn,), jnp.int32)

    outs = {k: [] for k in ("p_ckv", "p_kr", "p_sr", "p_si", "s_ckv", "s_kr", "s_sr", "s_si")}
    for l in range(depth):
        if l == 0:
            x, u, cq, kcat, ckv, kr, ckv_t = _inproj(
                (x_prompt.reshape(n_p, d), x_sample.reshape(n_s, d)), w_in_ext, l, g_q_r, g_kv_r, cs,
                ssm_w=ssm_w, ql=ql, kvl=kvl, rope=rope, tm=max(LANE, tm_tok // 2))
        else:
            x, u, cq, kcat, ckv, kr, ckv_t = _inproj(x2, w_in_ext, l, g_q_r, g_kv_r, cs, ssm_w=ssm_w, ql=ql,
                                                     kvl=kvl, rope=rope, tm=max(LANE, tm_tok // 2),
                                                     pre=(ffn_out, ln3g, ln3b, alpha))
        qcat = _qproj(cq, w_uq_ext, w_ukt, l, cs, heads=heads, nope=nope, kvl=kvl, scale=att_scale)

        o_p = _flash(qcat, kcat, ckv_t, batch=bp_, seq=tp, heads=heads, kvl=kvl)
        q_s = qcat[n_p:].reshape(bs, ts, heads, qw).transpose(0, 2, 1, 3).reshape(bs, heads * ts, qw)
        k_new = jnp.pad(kcat[n_p:].reshape(bs, ts, qw), ((0, 0), (0, kn_rows - ts), (0, 0)))
        o_s = _paged(page_table, q_s, k_new, cache_ckv, cache_krope_t, l, ts=ts, kvl=kvl, rope=rope)
        o_s = o_s.reshape(bs, heads, ts, kvl).transpose(0, 2, 1, 3).reshape(n_s, heads * kvl)

        y_p, f_p = _s5_scan_tok(u, s5_tok_blk, l, batch=bp_, seq=tp, nb=nbp)
        s0_s = jnp.concatenate([state_ssm_re[l].reshape(bs, -1), state_ssm_im[l].reshape(bs, -1)], axis=0)
        y2_s, f_s = _s5_scan(scan_rows(u[n_p:].reshape(bs, ts, ssm_w)), bs, s5_blk, l, s0_s)
        y2_s = y2_s.reshape(ts, 2, bs, ssm_w).transpose(1, 2, 0, 3).reshape(2, n_s, ssm_w)
        ssm_n = _s5_post(y_p.reshape(n_p, ssm_w), y2_s, u, d_r, w_glu_b, g_ssm_r, l, tm=tm_tok)

        x1, qm = _mixout(o_p, o_s, ssm_n, x, w_uv_h, g_mla_r, w_out_b, ln1g, ln1b, w_mq_b, l,
                         heads=heads, kvl=kvl, alpha=alpha, mscale=mem_scale, tm=tm_tok)

        c_p = _cross_prompt(qm, memk, memv, l, batch=bp_, seq=tp, mh=mh, md=md)
        c_s = _cross_sample(qm[n_p:], cache_k, cache_v, l, bs=bs, ts=ts, mh=mh, md=md)

        if l % 2 == 0:
            x2, = _crossout(c_p, c_s, x1, w_mo_b, ln2g, ln2b, l, alpha=alpha, tm=tm_tok)
            ffn_out = [_ffn(x2, ffg, ffu, ffd, dense_te(l // 2), dense_tv, tm_ffn, tf_dense)]
        else:
            x2, top_i, top_g = _crossout(c_p, c_s, x1, w_mo_b, ln2g, ln2b, l, alpha=alpha, tm=tm_tok,
                                         router=(wr_hi, wr_lo, l // 2, n_exp))
            tm_moe = 512
            src, gate, tile_e, tile_v, pos1, pos2 = _moe_plan(top_i, top_g, n_exp, tm_moe)
            take = lambda a, idx: a.at[idx].get(mode="promise_in_bounds")
            ys = _ffn(take(x2, src), eg, eu, ed, tile_e + (l // 2) * n_exp, tile_v, tm_moe, tf_moe,
                      scale=gate)
            ffn_out = [take(ys, pos1), take(ys, pos2)]

        outs["p_ckv"].append(ckv[:n_p].reshape(bp_, tp, kvl))
        outs["p_kr"].append(kr[:n_p].reshape(bp_, tp, rope))
        outs["p_sr"].append(f_p[:bp_].reshape(bp_, groups, pstate))
        outs["p_si"].append(f_p[nbp:nbp + bp_].reshape(bp_, groups, pstate))
        outs["s_ckv"].append(ckv[n_p:].reshape(bs, ts, kvl))
        outs["s_kr"].append(kr[n_p:].reshape(bs, ts, rope))
        outs["s_sr"].append(f_s[:bs].reshape(bs, groups, pstate))
        outs["s_si"].append(f_s[bs:].reshape(bs, groups, pstate))

    y_p, y_s = _addln(x2, ffn_out, ln3g, ln3b, depth - 1, alpha=alpha, tm=tm_tok, n_p=n_p)
    st = {k: jnp.stack(v) for k, v in outs.items()}
    return (y_p.reshape(bp_, tp, d), y_s.reshape(bs, ts, d),
            st["p_ckv"], st["p_kr"], st["p_sr"], st["p_si"],
            memk.reshape(depth, bp_, mlen, mh, md), memv.reshape(depth, bp_, mlen, mh, md),
            st["s_ckv"], st["s_kr"], st["s_sr"], st["s_si"])
```

```python
import functools
import math

import jax
import jax.numpy as jnp
from jax import lax
from jax.experimental import pallas as pl
from jax.experimental.pallas import tpu as pltpu

F32 = jnp.float32
BF16 = jnp.bfloat16

LANE = 128
SUBLANE = 8
VMEM_LIMIT = 56 * 2 ** 20
LN_EPS = 1e-5
RMS_EPS = 1e-6
ROPE_BASE = 10000.0
SLAB_GROUPS = 8
NEG = float(jnp.finfo(jnp.float32).min)


def _dot(a, b):
    return jnp.dot(a, b, preferred_element_type=F32)


def _dot_nt(a, b):
    return lax.dot_general(a, b, (((1,), (1,)), ((), ())), preferred_element_type=F32)


def _ln(v, g, b):
    vc = v - jnp.mean(v, -1, keepdims=True)
    var = jnp.mean(vc * vc, -1, keepdims=True)
    return vc * lax.rsqrt(var + LN_EPS) * g + b


def _rms(v, g):
    return v * lax.rsqrt(jnp.mean(v * v, -1, keepdims=True) + RMS_EPS) * g


def _tile(n, pref, mult=SUBLANE):
    best = None
    for t in range(mult, min(n, pref) + 1, mult):
        if n % t == 0:
            best = t
    assert best is not None, (n, pref, mult)
    return best


def _params(*sem):
    return pltpu.CompilerParams(dimension_semantics=sem, vmem_limit_bytes=VMEM_LIMIT)


def _rows(tm, w):
    return pl.BlockSpec((tm, w), lambda i: (i, 0))


def _const(shape):
    return pl.BlockSpec(shape, lambda i: (0,) * len(shape))


def _layer(l, shape):
    return pl.BlockSpec((None,) + shape, lambda i: (l,) + (0,) * len(shape))


def _rope_table_body(pos_ref, inv_ref, cs_ref):
    ang = pos_ref[...] * inv_ref[...]
    lane = lax.broadcasted_iota(jnp.int32, ang.shape, 1)
    c = jnp.cos(ang)
    s = jnp.sin(ang)
    cs_ref[...] = jnp.where(lane < LANE // 2, c, jnp.where(lane < 3 * LANE // 4, -s, s))


def _rope_table(pos, rope):
    n = pos.shape[0]
    half = rope // 2
    inv = ROPE_BASE ** (-jnp.arange(half, dtype=F32) / half)
    inv4 = jnp.tile(inv, 4)[None, :]
    tm = _tile(n, 1024)
    return pl.pallas_call(
        _rope_table_body, grid=(n // tm,),
        in_specs=[_rows(tm, 1), _const((1, LANE))], out_specs=_rows(tm, LANE),
        out_shape=jax.ShapeDtypeStruct((n, LANE), F32), compiler_params=_params("parallel"),
        name="rope_table")(pos, inv4)


def _inproj_body(*refs, ssm_w, ql, kvl, rope, n_y, alpha, npt):
    u_ref, cq_ref, kcat_ref, ckv_ref, kr_ref, ckvt_ref = refs[-6:]
    if npt:
        xp_ref, xs_ref, w_ref, gq_ref, gkv_ref, cs_ref, xo_ref = refs[:-6]
        x = jnp.where(pl.program_id(0) < npt, xp_ref[...], xs_ref[...])
        xo_ref[...] = x
    elif n_y:
        x_ref = refs[0]
        g_ref, b_ref, w_ref, gq_ref, gkv_ref, cs_ref, xo_ref = refs[1 + n_y:-6]
        y = refs[1][...]
        for r in refs[2:1 + n_y]:
            y = y + r[...]
        x = _ln(alpha * x_ref[...] + y, g_ref[...], b_ref[...])
        xo_ref[...] = x
    else:
        x_ref, w_ref, gq_ref, gkv_ref, cs_ref = refs[:-6]
        x = x_ref[...]
    xb = x.astype(BF16)
    o1, o2, o3 = ssm_w, ssm_w + ql, ssm_w + ql + kvl
    u_ref[...] = _dot(xb, w_ref[:, :o1])
    cq_ref[...] = _rms(_dot(xb, w_ref[:, o1:o2]), gq_ref[...]).astype(BF16)
    ckv = _rms(_dot(xb, w_ref[:, o2:o3]), gkv_ref[...])
    ckv_ref[...] = ckv
    ckvt_ref[...] = jnp.transpose(ckv).astype(BF16)
    kr2 = _dot(xb, w_ref[:, o3:o3 + LANE]) * cs_ref[...]
    kr = kr2 + pltpu.roll(kr2, LANE // 2, axis=1)
    kr_ref[...] = kr[:, :rope]
    kcat_ref[:, :kvl] = ckv.astype(BF16)
    kcat_ref[:, kvl:] = kr.astype(BF16)


def _inproj(x, w_ext, l, gq, gkv, cs, *, ssm_w, ql, kvl, rope, tm, pre=None):
    split = isinstance(x, tuple)
    assert not (split and pre is not None)
    n = x[0].shape[0] + x[1].shape[0] if split else x.shape[0]
    d = w_ext.shape[1]
    cols = w_ext.shape[2]
    assert tm % LANE == 0
    ys, alpha = ([], 0.0) if pre is None else (list(pre[0]), pre[3])
    npt = x[0].shape[0] // tm if split else 0
    body = functools.partial(_inproj_body, ssm_w=ssm_w, ql=ql, kvl=kvl, rope=rope, n_y=len(ys), alpha=alpha,
                             npt=npt)
    if split:
        in_specs = list(_split_rows(tm, d, npt))
        args = list(x)
    else:
        in_specs = [_rows(tm, d)] * (1 + len(ys))
        args = [x, *ys]
    out_specs, out_shape = [], []
    if split:
        out_specs.append(_rows(tm, d))
        out_shape.append(jax.ShapeDtypeStruct((n, d), F32))
    if pre is not None:
        in_specs += [_layer(l - 1, (1, d)), _layer(l - 1, (1, d))]
        args += [pre[1], pre[2]]
        out_specs.append(_rows(tm, d))
        out_shape.append(jax.ShapeDtypeStruct((n, d), F32))
    in_specs += [_layer(l, (d, cols)), _layer(l, (1, ql)), _layer(l, (1, kvl)), _rows(tm, LANE)]
    args += [w_ext, gq, gkv, cs]
    out_specs += [_rows(tm, ssm_w), _rows(tm, ql), _rows(tm, kvl + LANE), _rows(tm, kvl),
                  _rows(tm, rope), pl.BlockSpec((kvl, tm), lambda i: (0, i))]
    out_shape += [jax.ShapeDtypeStruct((n, ssm_w), F32), jax.ShapeDtypeStruct((n, ql), BF16),
                  jax.ShapeDtypeStruct((n, kvl + LANE), BF16), jax.ShapeDtypeStruct((n, kvl), F32),
                  jax.ShapeDtypeStruct((n, rope), F32), jax.ShapeDtypeStruct((kvl, n), BF16)]
    return pl.pallas_call(
        body, grid=(n // tm,), in_specs=in_specs, out_specs=out_specs, out_shape=out_shape,
        compiler_params=_params("parallel"), name="in_proj")(*args)


def _qproj_body(cq_ref, wuq_ref, wukt_ref, cs_ref, q_ref, *, heads, nope, kvl, scale):
    q = _dot(cq_ref[...], wuq_ref[...])
    cs = cs_ref[...]
    lane = lax.broadcasted_iota(jnp.int32, cs.shape, 1)
    hw = nope + LANE
    qw = kvl + LANE
    for h in range(heads):
        qlat = _dot(q[:, h * hw:h * hw + nope].astype(BF16), wukt_ref[h])
        q2 = q[:, h * hw + nope:(h + 1) * hw] * cs
        qr = q2 + pltpu.roll(q2, LANE // 2, axis=1)
        qr = jnp.where(lane < LANE // 2, qr, 0.0)
        q_ref[:, h * qw:h * qw + kvl] = (qlat * scale).astype(BF16)
        q_ref[:, h * qw + kvl:(h + 1) * qw] = (qr * scale).astype(BF16)


def _qproj(cq, wuq_ext, wukt, l, cs, *, heads, nope, kvl, scale):
    n, ql = cq.shape
    hw = nope + LANE
    qw = kvl + LANE
    tm = _tile(n, 512)
    body = functools.partial(_qproj_body, heads=heads, nope=nope, kvl=kvl, scale=scale)
    return pl.pallas_call(
        body, grid=(n // tm,),
        in_specs=[_rows(tm, ql), _layer(l, (ql, heads * hw)), _layer(l, (heads, nope, kvl)),
                  _rows(tm, LANE)],
        out_specs=_rows(tm, heads * qw),
        out_shape=jax.ShapeDtypeStruct((n, heads * qw), BF16),
        compiler_params=_params("parallel"), name="q_proj")(cq, wuq_ext, wukt, cs)


def _flash_body(q_ref, k_ref, vt_ref, o_ref, m_ref, l_ref, acc_ref, *, heads, kvl, qw):
    i = pl.program_id(1)
    j = pl.program_id(2)

    @pl.when(j == 0)
    def _():
        m_ref[...] = jnp.full(m_ref.shape, NEG, F32)
        l_ref[...] = jnp.zeros(l_ref.shape, F32)
        acc_ref[...] = jnp.zeros(acc_ref.shape, F32)

    def step(masked):
        k = k_ref[...]
        vt = vt_ref[...]
        for h in range(heads):
            s = _dot_nt(k, q_ref[:, h * qw:(h + 1) * qw])
            if masked:
                key = lax.broadcasted_iota(jnp.int32, s.shape, 0)
                qry = lax.broadcasted_iota(jnp.int32, s.shape, 1)
                s = jnp.where(key <= qry, s, NEG)
            m_prev = m_ref[h]
            m_new = jnp.maximum(m_prev, jnp.max(s, axis=0, keepdims=True))
            a = jnp.exp2(m_prev - m_new)
            p = jnp.exp2(s - m_new)
            l_ref[h] = a * l_ref[h] + jnp.sum(p, axis=0, keepdims=True)
            acc_ref[h] = a * acc_ref[h] + _dot(vt, p.astype(BF16))
            m_ref[h] = m_new

    @pl.when(j < i)
    def _():
        step(False)

    @pl.when(j == i)
    def _():
        step(True)
        for h in range(heads):
            o_ref[:, h * kvl:(h + 1) * kvl] = jnp.transpose(acc_ref[h] / l_ref[h]).astype(BF16)


def _flash(qcat, kcat, ckv_t, *, batch, seq, heads, kvl):
    qw = kvl + LANE
    tq = _tile(seq, 512, mult=LANE)
    nq = seq // tq
    body = functools.partial(_flash_body, heads=heads, kvl=kvl, qw=qw)
    return pl.pallas_call(
        body, grid=(batch, nq, nq),
        in_specs=[pl.BlockSpec((tq, heads * qw), lambda b, i, j: (b * nq + i, 0)),
                  pl.BlockSpec((tq, qw), lambda b, i, j: (b * nq + jnp.minimum(i, j), 0)),
                  pl.BlockSpec((kvl, tq), lambda b, i, j: (0, b * nq + jnp.minimum(i, j)))],
        out_specs=pl.BlockSpec((tq, heads * kvl), lambda b, i, j: (b * nq + i, 0)),
        out_shape=jax.ShapeDtypeStruct((batch * seq, heads * kvl), BF16),
        scratch_shapes=[pltpu.VMEM((heads, 1, tq), F32), pltpu.VMEM((heads, 1, tq), F32),
                        pltpu.VMEM((heads, kvl, tq), F32)],
        compiler_params=_params("parallel", "parallel", "arbitrary"), name="prompt_attn")(
            qcat, kcat, ckv_t)


def _paged_body(pt_ref, q_ref, kn_ref, ckv_hbm, krt_hbm, o_ref, ckv_buf, krt_buf, sem, s_ref, kb_ref,
                *, layer, n_pages, page, kvl, rope, ts, chunk_pages):
    b = pl.program_id(0)
    nb = pl.num_programs(0)
    slot = lax.rem(b, 2)

    def fetch(bb, sl):
        def issue(i, c):
            for k in range(2):
                p = 2 * i + k
                pg = pt_ref[bb * n_pages + p]
                pltpu.make_async_copy(ckv_hbm.at[layer, pg], ckv_buf.at[sl, p], sem.at[sl, 0]).start(k)
                pltpu.make_async_copy(krt_hbm.at[layer, pg], krt_buf.at[sl, p], sem.at[sl, 1]).start(1 - k)
            return c
        lax.fori_loop(0, n_pages // 2, issue, 0, unroll=4)

    @pl.when(b == 0)
    def _():
        fetch(0, 0)

    @pl.when(b + 1 < nb)
    def _():
        fetch(b + 1, 1 - slot)

    pltpu.make_async_copy(ckv_buf.at[slot], ckv_buf.at[slot], sem.at[slot, 0]).wait()
    pltpu.make_async_copy(krt_buf.at[slot], krt_buf.at[slot], sem.at[slot, 1]).wait()

    q = q_ref[0]
    ql = q[:, :kvl]
    qr = q[:, kvl:kvl + rope]
    ck = chunk_pages * page
    n_chunks = n_pages // chunk_pages
    for c in range(n_chunks):
        kc = ckv_buf[slot, pl.ds(c * chunk_pages, chunk_pages)].reshape(ck, kvl).astype(BF16)
        rt = jnp.concatenate([krt_buf[slot, c * chunk_pages + p] for p in range(chunk_pages)],
                             axis=1).astype(BF16)
        s_ref[:, c * ck:(c + 1) * ck] = _dot_nt(ql, kc) + _dot(qr, rt)
        kb_ref[c * ck:(c + 1) * ck, :] = kc

    kn = kn_ref[0]
    sn = _dot_nt(q, kn)
    qj = lax.broadcasted_iota(jnp.int32, sn.shape, 0) & (ts - 1)
    col = lax.broadcasted_iota(jnp.int32, sn.shape, 1)
    sn = jnp.where(col <= qj, sn, NEG)

    m = jnp.maximum(jnp.max(s_ref[...], axis=1, keepdims=True), jnp.max(sn, axis=1, keepdims=True))
    pn = jnp.exp2(sn - m)
    l = jnp.sum(pn, axis=1, keepdims=True)
    acc = _dot(pn.astype(BF16), kn[:, :kvl])
    for c in range(n_chunks):
        p = jnp.exp2(s_ref[:, c * ck:(c + 1) * ck] - m)
        l = l + jnp.sum(p, axis=1, keepdims=True)
        acc = acc + _dot(p.astype(BF16), kb_ref[c * ck:(c + 1) * ck, :])
    o_ref[0] = (acc / l).astype(BF16)


def _paged(page_table, q_s, k_new, cache_ckv, cache_krope_t, l, *, ts, kvl, rope):
    bs, rows, qw = q_s.shape
    n_pages = page_table.shape[1]
    page = cache_ckv.shape[2]
    assert cache_krope_t.shape[2:] == (rope, page) and n_pages % 2 == 0
    chunk_pages = _tile(n_pages, 16, mult=1)
    past = n_pages * page
    assert ts & (ts - 1) == 0
    body = functools.partial(_paged_body, layer=l, n_pages=n_pages, page=page, kvl=kvl, rope=rope,
                             ts=ts, chunk_pages=chunk_pages)
    grid_spec = pltpu.PrefetchScalarGridSpec(
        num_scalar_prefetch=1, grid=(bs,),
        in_specs=[pl.BlockSpec((1, rows, qw), lambda b, pt: (b, 0, 0)),
                  pl.BlockSpec((1, k_new.shape[1], qw), lambda b, pt: (b, 0, 0)),
                  pl.BlockSpec(memory_space=pl.ANY), pl.BlockSpec(memory_space=pl.ANY)],
        out_specs=pl.BlockSpec((1, rows, kvl), lambda b, pt: (b, 0, 0)),
        scratch_shapes=[pltpu.VMEM((2, n_pages, page, kvl), F32),
                        pltpu.VMEM((2, n_pages, rope, page), F32),
                        pltpu.SemaphoreType.DMA((2, 2)),
                        pltpu.VMEM((rows, past), F32),
                        pltpu.VMEM((past, kvl), BF16)])
    return pl.pallas_call(
        body, grid_spec=grid_spec, out_shape=jax.ShapeDtypeStruct((bs, rows, kvl), BF16),
        compiler_params=_params("arbitrary"), name="paged_attn")(
            page_table.reshape(-1), q_s, k_new, cache_ckv, cache_krope_t)


def _s5_prep_body(lr_ref, li_ref, ldt_ref, br_ref, bi_ref, ar_ref, ai_ref, bbr_ref, bbi_ref):
    lr = lr_ref[...]
    li = li_ref[...]
    dt = jnp.exp(ldt_ref[...])
    mag = jnp.exp(lr * dt)
    ar = mag * jnp.cos(li * dt)
    ai = mag * jnp.sin(li * dt)
    den = lr * lr + li * li
    zr = ((ar - 1.0) * lr + ai * li) / den
    zi = (ai * lr - (ar - 1.0) * li) / den
    ar_ref[...] = ar
    ai_ref[...] = ai
    br = br_ref[...]
    bi = bi_ref[...]
    bbr_ref[...] = zr * br - zi * bi
    bbi_ref[...] = zr * bi + zi * br


def _s5_prep(lam_re, lam_im, log_dt, b_re_t, b_im_t):
    nl, g, _, p = lam_re.shape
    c = b_re_t.shape[2]
    gp = pl.BlockSpec((None, g, 1, p), lambda l: (l, 0, 0, 0))
    gcp = pl.BlockSpec((None, g, c, p), lambda l: (l, 0, 0, 0))
    return pl.pallas_call(
        _s5_prep_body, grid=(nl,),
        in_specs=[gp, gp, pl.BlockSpec((None, g, 1, 1), lambda l: (l, 0, 0, 0)), gcp, gcp],
        out_specs=[gp, gp, gcp, gcp],
        out_shape=[jax.ShapeDtypeStruct((nl, g, 1, p), F32)] * 2
        + [jax.ShapeDtypeStruct((nl, g, c, p), F32)] * 2,
        compiler_params=_params("parallel"), name="s5_prep")(lam_re, lam_im, log_dt, b_re_t, b_im_t)


def _swap_halves(z, nb):
    if 2 * nb == SUBLANE:
        return pltpu.roll(z, nb, axis=0)
    return jnp.concatenate([z[nb:], z[:nb]], axis=0)


def _s5_scan_body(u_ref, bw_ref, cw_ref, ar_ref, ai_ref, s0_ref, y_ref, f_ref, x_ref, st_ref,
                  *, nb, tc, nslab, sc, sp, lw):
    step_rows = 2 * nb
    w = nslab * sc

    @pl.when(pl.program_id(0) == 0)
    def _():
        st_ref[...] = s0_ref[...]

    for j in range(nslab):
        lhs = jnp.concatenate([u_ref[:, j * sc:(j + 1) * sc], u_ref[:, w + j * sc:w + (j + 1) * sc]], axis=1)
        x_ref[:, j * sp:(j + 1) * sp] = _dot(lhs, bw_ref[j])

    im_row = lax.broadcasted_iota(jnp.int32, (step_rows, lw), 0) >= nb
    for p0 in range(0, nslab * sp, lw):
        lanes = slice(p0, p0 + lw)
        a1 = jnp.broadcast_to(ar_ref[:, lanes], (step_rows, lw))
        ai = jnp.broadcast_to(ai_ref[:, lanes], (step_rows, lw))
        a2 = jnp.where(im_row, ai, -ai)
        z = st_ref[:, lanes]
        for t in range(tc):
            rows = slice(t * step_rows, (t + 1) * step_rows)
            z = a1 * z + a2 * _swap_halves(z, nb) + x_ref[rows, lanes]
            x_ref[rows, lanes] = z
        st_ref[:, lanes] = z

    re_row = (lax.broadcasted_iota(jnp.int32, (tc * step_rows, sc), 0) & nb) == 0
    for j in range(nslab):
        y2 = _dot(x_ref[:, j * sp:(j + 1) * sp].astype(BF16), cw_ref[j])
        y_ref[:, j * sc:(j + 1) * sc] = jnp.where(re_row, y2[:, :sc], y2[:, sc:])
    f_ref[...] = st_ref[...]


def _s5_scan(u2, nb, blk, l, s0):
    bw, cw, ar, ai = blk
    rows, w2 = u2.shape
    nslab, sc2, sp = bw.shape[1:]
    sc = sc2 // 2
    w = w2 // 2
    gp = nslab * sp
    step_rows = 2 * nb
    assert nb & (nb - 1) == 0 and step_rows % SUBLANE == 0
    steps = rows // step_rows
    tc = _tile(steps, max(1, 256 // step_rows), mult=1)
    rc = tc * step_rows
    lw = _tile(gp, max(LANE, (8 * SUBLANE * LANE) // step_rows), mult=LANE)
    body = functools.partial(_s5_scan_body, nb=nb, tc=tc, nslab=nslab, sc=sc, sp=sp, lw=lw)
    return pl.pallas_call(
        body, grid=(steps // tc,),
        in_specs=[_rows(rc, w2), _layer(l, (nslab, sc2, sp)), _layer(l, (nslab, sp, sc2)),
                  _layer(l, (1, gp)), _layer(l, (1, gp)), _const((step_rows, gp))],
        out_specs=[_rows(rc, w), _const((step_rows, gp))],
        out_shape=[jax.ShapeDtypeStruct((rows, w), F32), jax.ShapeDtypeStruct((step_rows, gp), F32)],
        scratch_shapes=[pltpu.VMEM((rc, gp), F32), pltpu.VMEM((step_rows, gp), F32)],
        compiler_params=_params("arbitrary"), name="s5_scan")(u2, bw, cw, ar, ai, s0)


def _s5_scan_tok_body(*refs, nreal, nb, tc, pitch, nslab, sc, sp, lw):
    u_refs = refs[:nreal]
    (bre_ref, bim_ref, cre_ref, cim_ref, a1_ref, a2_ref, y_ref, f_ref, ub_ref, x_ref, st_ref) = refs[nreal:]
    step_rows = 2 * nb
    half = nb * pitch
    nq = sp // LANE

    @pl.when(pl.program_id(0) == 0)
    def _():
        st_ref[...] = jnp.zeros(st_ref.shape, F32)
        ub_ref[...] = jnp.zeros(ub_ref.shape, F32)

    for b in range(nreal):
        ub_ref[b * pitch:b * pitch + tc, :] = u_refs[b][...]
    for j in range(nslab):
        uj = ub_ref[:, j * sc:(j + 1) * sc].astype(BF16)
        xr = _dot(uj, bre_ref[j])
        xi = _dot(uj, bim_ref[j])
        for q in range(nq):
            x_ref[j * nq + q, 0:half, :] = xr[:, q * LANE:(q + 1) * LANE]
            x_ref[j * nq + q, half:2 * half, :] = xi[:, q * LANE:(q + 1) * LANE]

    npass = nslab * sp // lw
    nl = lw // LANE
    zs = [st_ref[:, p * lw:(p + 1) * lw] for p in range(npass)]
    for t in range(tc):
        rows = pl.ds(t, step_rows, stride=pitch)
        for p in range(npass):
            lanes = slice(p * lw, (p + 1) * lw)
            x_t = jnp.concatenate([x_ref[p * nl + q, rows, :] for q in range(nl)], axis=1)
            z = a1_ref[:, lanes] * zs[p] + a2_ref[:, lanes] * _swap_halves(zs[p], nb) + x_t
            for q in range(nl):
                x_ref[p * nl + q, rows, :] = z[:, q * LANE:(q + 1) * LANE]
            zs[p] = z
    for p in range(npass):
        st_ref[:, p * lw:(p + 1) * lw] = zs[p]

    for j in range(nslab):
        sr = jnp.concatenate([x_ref[j * nq + q, 0:half, :] for q in range(nq)], axis=1).astype(BF16)
        si = jnp.concatenate([x_ref[j * nq + q, half:2 * half, :] for q in range(nq)], axis=1).astype(BF16)
        y = _dot(sr, cre_ref[j]) - _dot(si, cim_ref[j])
        for b in range(nreal):
            y_ref[b, :, j * sc:(j + 1) * sc] = y[b * pitch:b * pitch + tc]
    f_ref[...] = st_ref[...]


def _s5_scan_tok(u, blk, l, *, batch, seq, nb):
    bre, bim, cre, cim, a1, a2 = blk
    w = u.shape[1]
    nslab, sc, sp = bre.shape[1:]
    gp = nslab * sp
    step_rows = 2 * nb
    assert step_rows == SUBLANE and batch <= nb
    tc = _tile(seq, 64, mult=2 * SUBLANE)
    pitch = tc + SUBLANE // 2
    nt = seq // tc
    lw = _tile(gp, 8 * LANE, mult=LANE)
    body = functools.partial(_s5_scan_tok_body, nreal=batch, nb=nb, tc=tc, pitch=pitch, nslab=nslab,
                             sc=sc, sp=sp, lw=lw)
    u_specs = [pl.BlockSpec((tc, w), functools.partial(lambda i, b: (b * nt + i, 0), b=b))
               for b in range(batch)]
    return pl.pallas_call(
        body, grid=(nt,),
        in_specs=u_specs + [_layer(l, (nslab, sc, sp)), _layer(l, (nslab, sc, sp)),
                            _layer(l, (nslab, sp, sc)), _layer(l, (nslab, sp, sc)),
                            _layer(l, (step_rows, gp)), _layer(l, (step_rows, gp))],
        out_specs=[pl.BlockSpec((batch, tc, w), lambda i: (0, i, 0)), _const((step_rows, gp))],
        out_shape=[jax.ShapeDtypeStruct((batch, seq, w), F32), jax.ShapeDtypeStruct((step_rows, gp), F32)],
        scratch_shapes=[pltpu.VMEM((nb * pitch, w), F32), pltpu.VMEM((gp // LANE, step_rows * pitch, LANE), F32),
                        pltpu.VMEM((step_rows, gp), F32)],
        compiler_params=_params("arbitrary"), name="s5_scan_tok")(
            *([u] * batch), bre, bim, cre, cim, a1, a2)


def _s5_post_body(yp_ref, sre_ref, sim_ref, u_ref, d_ref, wglu_ref, g_ref, o_ref, *, npt):
    y = jnp.where(pl.program_id(0) < npt, yp_ref[...], sre_ref[...] + sim_ref[...])
    yf = y + d_ref[...] * u_ref[...]
    g = jax.nn.gelu(yf)
    z = g * jax.nn.sigmoid(_dot(g.astype(BF16), wglu_ref[...]))
    o_ref[...] = _rms(z, g_ref[...]).astype(BF16)


def _s5_post(y_p, y_s, u, d, wglu, g, l, *, tm):
    n, w = u.shape
    npt = y_p.shape[0] // tm

    def smap(part):
        return pl.BlockSpec((None, tm, w), lambda i: (part, jnp.maximum(i - npt, 0), 0))

    return pl.pallas_call(
        functools.partial(_s5_post_body, npt=npt), grid=(n // tm,),
        in_specs=[_split_rows(tm, w, npt)[0], smap(0), smap(1), _rows(tm, w), _layer(l, (1, w)),
                  _layer(l, (w, w)), _layer(l, (1, w))],
        out_specs=_rows(tm, w), out_shape=jax.ShapeDtypeStruct((n, w), BF16),
        compiler_params=_params("parallel"), name="s5_post")(y_p, y_s, y_s, u, d, wglu, g)


def _split_rows(tm, w, npt):
    return (pl.BlockSpec((tm, w), lambda i: (jnp.minimum(i, npt - 1), 0)),
            pl.BlockSpec((tm, w), lambda i: (jnp.maximum(i - npt, 0), 0)))


def _mixout_body(op_ref, os_ref, ssm_ref, x_ref, wuv_ref, gm_ref, wout_ref, g1_ref, b1_ref, wmq_ref,
                 x1_ref, qm_ref, *, heads, kvl, ssm_w, alpha, mscale, npt):
    is_prompt = pl.program_id(0) < npt

    def o_head(h):
        cols = slice(h * kvl, (h + 1) * kvl)
        return jnp.where(is_prompt, op_ref[:, cols], os_ref[:, cols])

    mla = jnp.concatenate([_dot(o_head(h), wuv_ref[h]) for h in range(heads)], axis=-1)
    mla_n = _rms(mla, gm_ref[...]).astype(BF16)
    m = _dot(ssm_ref[...], wout_ref[:ssm_w, :]) + _dot(mla_n, wout_ref[ssm_w:, :])
    x1 = _ln(alpha * x_ref[...] + m, g1_ref[...], b1_ref[...])
    x1_ref[...] = x1
    qm_ref[...] = (_dot(x1.astype(BF16), wmq_ref[...]) * mscale).astype(BF16)


def _mixout(o_p, o_s, ssm_n, x, wuv, gm, wout, g1, b1, wmq, l, *, heads, kvl, alpha, mscale, tm):
    n, d = x.shape
    ssm_w = ssm_n.shape[1]
    v = wuv.shape[3]
    mix = wout.shape[1]
    mq = wmq.shape[2]
    npt = o_p.shape[0] // tm
    body = functools.partial(_mixout_body, heads=heads, kvl=kvl, ssm_w=ssm_w, alpha=alpha, mscale=mscale,
                             npt=npt)
    return pl.pallas_call(
        body, grid=(n // tm,),
        in_specs=[*_split_rows(tm, heads * kvl, npt), _rows(tm, ssm_w), _rows(tm, d),
                  _layer(l, (heads, kvl, v)), _layer(l, (1, heads * v)), _layer(l, (mix, d)),
                  _layer(l, (1, d)), _layer(l, (1, d)), _layer(l, (d, mq))],
        out_specs=[_rows(tm, d), _rows(tm, mq)],
        out_shape=[jax.ShapeDtypeStruct((n, d), F32), jax.ShapeDtypeStruct((n, mq), BF16)],
        compiler_params=_params("parallel"), name="mix_out")(
            o_p, o_s, ssm_n, x, wuv, gm, wout, g1, b1, wmq)


def _memkv_body(mem_ref, wk_ref, wv_ref, k_ref, v_ref):
    mb = mem_ref[...].astype(BF16)
    k_ref[...] = _dot(mb, wk_ref[...])
    v_ref[...] = _dot(mb, wv_ref[...])


def _memkv(mem, wk, wv):
    nl, d, hk = wk.shape
    rows = mem.shape[0]
    w = pl.BlockSpec((None, d, hk), lambda l: (l, 0, 0))
    o = pl.BlockSpec((None, rows, hk), lambda l: (l, 0, 0))
    return pl.pallas_call(
        _memkv_body, grid=(nl,), in_specs=[_const((rows, d)), w, w], out_specs=[o, o],
        out_shape=[jax.ShapeDtypeStruct((nl, rows, hk), F32)] * 2,
        compiler_params=_params("parallel"), name="mem_kv")(mem, wk, wv)


def _softmax_rows(s):
    p = jnp.exp(s - jnp.max(s, axis=1, keepdims=True))
    return p / jnp.sum(p, axis=1, keepdims=True)


def _cross_p_body(q_ref, mk_ref, mv_ref, o_ref, *, mh, md):
    mk = mk_ref[...].astype(BF16)
    mv = mv_ref[...].astype(BF16)
    for h in range(mh):
        cols = slice(h * md, (h + 1) * md)
        p = _softmax_rows(_dot_nt(q_ref[:, cols], mk[:, cols]))
        o_ref[:, cols] = _dot(p.astype(BF16), mv[:, cols]).astype(BF16)


def _cross_prompt(qm, memk, memv, l, *, batch, seq, mh, md):
    mlen = memk.shape[1] // batch
    hk = mh * md
    tm = _tile(seq, 512)
    nt = seq // tm
    kv = pl.BlockSpec((None, mlen, hk), lambda b, i: (l, b, 0))
    body = functools.partial(_cross_p_body, mh=mh, md=md)
    return pl.pallas_call(
        body, grid=(batch, nt),
        in_specs=[pl.BlockSpec((tm, hk), lambda b, i: (b * nt + i, 0)), kv, kv],
        out_specs=pl.BlockSpec((tm, hk), lambda b, i: (b * nt + i, 0)),
        out_shape=jax.ShapeDtypeStruct((batch * seq, hk), BF16),
        compiler_params=_params("parallel", "parallel"), name="cross_prompt")(qm, memk, memv)


def _cross_s_body(q_ref, mk_ref, mv_ref, o_ref, *, bb, ts, mh, md):
    q_all = q_ref[...].astype(F32)
    rows_kv = mk_ref.shape[1]
    row_head = lax.broadcasted_iota(jnp.int32, (mh * ts, rows_kv), 0) >> (ts.bit_length() - 1)
    col_head = lax.broadcasted_iota(jnp.int32, (mh * ts, rows_kv), 1) & (mh - 1)
    own = row_head == col_head
    outs = []
    for b in range(bb):
        q = q_all[b * ts:(b + 1) * ts, :]
        qh = jnp.concatenate([q[:, h * md:(h + 1) * md] for h in range(mh)], axis=0).astype(BF16)
        s = jnp.where(own, _dot_nt(qh, mk_ref[b].astype(BF16)), NEG)
        o = _dot(_softmax_rows(s).astype(BF16), mv_ref[b].astype(BF16))
        outs.append(jnp.concatenate([o[h * ts:(h + 1) * ts] for h in range(mh)], axis=1))
    o_ref[...] = jnp.concatenate(outs, axis=0).astype(BF16)


def _cross_sample(qm_s, cache_k, cache_v, l, *, bs, ts, mh, md):
    rows_kv = cache_k.shape[2]
    hk = mh * md
    assert mh & (mh - 1) == 0 and ts & (ts - 1) == 0
    bb = _tile(bs, 8, mult=1)
    kv = pl.BlockSpec((None, bb, rows_kv, md), lambda i: (l, i, 0, 0))
    body = functools.partial(_cross_s_body, bb=bb, ts=ts, mh=mh, md=md)
    return pl.pallas_call(
        body, grid=(bs // bb,),
        in_specs=[_rows(bb * ts, hk), kv, kv], out_specs=_rows(bb * ts, hk),
        out_shape=jax.ShapeDtypeStruct((bs * ts, hk), BF16),
        compiler_params=_params("parallel"), name="cross_sample")(qm_s, cache_k, cache_v)


def _top2(lt, n_exp):
    e = jnp.exp(lt - jnp.max(lt, axis=0, keepdims=True))
    probs = e / jnp.sum(e, axis=0, keepdims=True)
    eidx = lax.broadcasted_iota(jnp.int32, probs.shape, 0)
    m1 = jnp.max(probs, axis=0, keepdims=True)
    i1 = jnp.min(jnp.where(probs == m1, eidx, n_exp), axis=0, keepdims=True)
    rest = jnp.where(eidx == i1, -1.0, probs)
    m2 = jnp.max(rest, axis=0, keepdims=True)
    i2 = jnp.min(jnp.where(rest == m2, eidx, n_exp), axis=0, keepdims=True)
    den = m1 + m2
    return i1, i2, m1 / den, m2 / den


def _crossout_body(*refs, alpha, n_exp, npt):
    if n_exp:
        op_ref, os_ref, x1_ref, wmo_ref, g_ref, b_ref, wrh_ref, wrl_ref, x2_ref, ti_ref, tg_ref = refs
    else:
        op_ref, os_ref, x1_ref, wmo_ref, g_ref, b_ref, x2_ref = refs
    o = jnp.where(pl.program_id(0) < npt, op_ref[...], os_ref[...])
    x2 = _ln(alpha * x1_ref[...] + _dot(o, wmo_ref[...]), g_ref[...], b_ref[...])
    x2_ref[...] = x2
    if n_exp:
        xh = x2.astype(BF16)
        xl = (x2 - xh.astype(F32)).astype(BF16)
        wh = wrh_ref[...]
        lt = _dot_nt(wh, xh) + (_dot_nt(wh, xl) + _dot_nt(wrl_ref[...], xh))
        i1, i2, g1, g2 = _top2(lt[:n_exp], n_exp)
        ti_ref[0:1, :] = i1
        ti_ref[1:2, :] = i2
        tg_ref[0:1, :] = g1
        tg_ref[1:2, :] = g2


def _crossout(o_p, o_s, x1, wmo, g, b, l, *, alpha, tm, router=None):
    n, d = x1.shape
    hk = o_p.shape[1]
    assert tm % LANE == 0
    npt = o_p.shape[0] // tm
    in_specs = [*_split_rows(tm, hk, npt), _rows(tm, d), _layer(l, (hk, d)), _layer(l, (1, d)),
                _layer(l, (1, d))]
    out_specs = [_rows(tm, d)]
    out_shape = [jax.ShapeDtypeStruct((n, d), F32)]
    args = [o_p, o_s, x1, wmo, g, b]
    n_exp = 0
    if router is not None:
        wrh, wrl, li, n_exp = router
        er = wrh.shape[1]
        in_specs += [_layer(li, (er, d)), _layer(li, (er, d))]
        top = pl.BlockSpec((2, tm), lambda i: (0, i))
        out_specs += [top, top]
        out_shape += [jax.ShapeDtypeStruct((2, n), jnp.int32), jax.ShapeDtypeStruct((2, n), F32)]
        args += [wrh, wrl]
    body = functools.partial(_crossout_body, alpha=alpha, n_exp=n_exp, npt=npt)
    return pl.pallas_call(
        body, grid=(n // tm,), in_specs=in_specs, out_specs=out_specs, out_shape=out_shape,
        compiler_params=_params("parallel"), name="cross_out")(*args)


def _ffn_body(te_ref, tv_ref, *refs, scaled):
    if scaled:
        x_ref, wg_ref, wu_ref, wd_ref, sc_ref, o_ref, xb_ref = refs
    else:
        x_ref, wg_ref, wu_ref, wd_ref, o_ref, xb_ref = refs
    i = pl.program_id(0)
    j = pl.program_id(1)
    last = pl.num_programs(1) - 1

    @pl.when(tv_ref[i] > 0)
    def _():
        @pl.when(j == 0)
        def _():
            xb_ref[...] = x_ref[...].astype(BF16)
            o_ref[...] = jnp.zeros(o_ref.shape, F32)

        x = xb_ref[...]
        h = (jax.nn.silu(_dot(x, wg_ref[...])) * _dot(x, wu_ref[...])).astype(BF16)
        o_ref[...] += _dot(h, wd_ref[...])

        if scaled:
            @pl.when(j == last)
            def _():
                o_ref[...] *= sc_ref[...]

    @pl.when(tv_ref[i] == 0)
    def _():
        o_ref[...] = jnp.zeros(o_ref.shape, F32)


def _ffn(xs, wg, wu, wd, tile_expert, tile_valid, tm, tf, scale=None):
    rows, d = xs.shape
    ff = wg.shape[2]
    in_specs = [pl.BlockSpec((tm, d), lambda i, j, te, tv: (i, 0)),
                pl.BlockSpec((None, d, tf), lambda i, j, te, tv: (te[i], 0, j)),
                pl.BlockSpec((None, d, tf), lambda i, j, te, tv: (te[i], 0, j)),
                pl.BlockSpec((None, tf, d), lambda i, j, te, tv: (te[i], j, 0))]
    args = [xs, wg, wu, wd]
    if scale is not None:
        in_specs.append(pl.BlockSpec((tm, 1), lambda i, j, te, tv: (i, 0)))
        args.append(scale)
    grid_spec = pltpu.PrefetchScalarGridSpec(
        num_scalar_prefetch=2, grid=(rows // tm, ff // tf), in_specs=in_specs,
        out_specs=pl.BlockSpec((tm, d), lambda i, j, te, tv: (i, 0)),
        scratch_shapes=[pltpu.VMEM((tm, d), BF16)])
    return pl.pallas_call(
        functools.partial(_ffn_body, scaled=scale is not None), grid_spec=grid_spec,
        out_shape=jax.ShapeDtypeStruct((rows, d), F32),
        compiler_params=_params("parallel", "arbitrary"), name="ffn")(tile_expert, tile_valid, *args)


def _addln_body(*refs, alpha, n_y, npt):
    x_ref = refs[0]
    y_refs = refs[1:1 + n_y]
    g_ref, b_ref, op_ref, os_ref = refs[1 + n_y:]
    y = y_refs[0][...]
    for r in y_refs[1:]:
        y = y + r[...]
    out = _ln(alpha * x_ref[...] + y, g_ref[...], b_ref[...])

    @pl.when(pl.program_id(0) < npt)
    def _():
        op_ref[...] = out

    @pl.when(pl.program_id(0) >= npt)
    def _():
        os_ref[...] = out


def _addln(x, ys, g, b, l, *, alpha, tm, n_p):
    n, d = x.shape
    npt = n_p // tm
    body = functools.partial(_addln_body, alpha=alpha, n_y=len(ys), npt=npt)
    return pl.pallas_call(
        body, grid=(n // tm,),
        in_specs=[_rows(tm, d)] * (1 + len(ys)) + [_layer(l, (1, d)), _layer(l, (1, d))],
        out_specs=list(_split_rows(tm, d, npt)),
        out_shape=[jax.ShapeDtypeStruct((n_p, d), F32), jax.ShapeDtypeStruct((n - n_p, d), F32)],
        compiler_params=_params("arbitrary"), name="add_ln")(x, *ys, g, b)


def _cast_cols_body(w_ref, o_ref, *, f):
    o_ref[:, :f] = w_ref[...].astype(BF16)
    if o_ref.shape[1] > f:
        o_ref[:, f:] = jnp.zeros((o_ref.shape[0], o_ref.shape[1] - f), BF16)


def _cast_pad_cols(w, fp):
    e, d, f = w.shape
    td = _tile(d, 512)
    return pl.pallas_call(
        functools.partial(_cast_cols_body, f=f), grid=(e, d // td),
        in_specs=[pl.BlockSpec((None, td, f), lambda a, b: (a, b, 0))],
        out_specs=pl.BlockSpec((None, td, fp), lambda a, b: (a, b, 0)),
        out_shape=jax.ShapeDtypeStruct((e, d, fp), BF16),
        compiler_params=_params("parallel", "parallel"), name="cast_cols")(w)


def _cast_rows_body(w_ref, o_ref, *, f):
    o_ref[:f, :] = w_ref[...].astype(BF16)
    if o_ref.shape[0] > f:
        o_ref[f:, :] = jnp.zeros((o_ref.shape[0] - f, o_ref.shape[1]), BF16)


def _cast_pad_rows(w, fp):
    e, f, d = w.shape
    assert f % (2 * SUBLANE) == 0
    dc = _tile(d, 512, mult=LANE)
    return pl.pallas_call(
        functools.partial(_cast_rows_body, f=f), grid=(e, d // dc),
        in_specs=[pl.BlockSpec((None, f, dc), lambda a, b: (a, 0, b))],
        out_specs=pl.BlockSpec((None, fp, dc), lambda a, b: (a, 0, b)),
        out_shape=jax.ShapeDtypeStruct((e, fp, d), BF16),
        compiler_params=_params("parallel", "parallel"), name="cast_rows")(w)


def _moe_plan(top_i, top_g, n_exp, tm):
    n = top_i.shape[1]
    flat_e = top_i.reshape(-1)
    onehot = (flat_e[:, None] == jnp.arange(n_exp, dtype=jnp.int32)[None, :]).astype(jnp.int32)
    csum = jnp.cumsum(onehot, axis=0)
    rank = jnp.sum(onehot * csum, axis=1) - 1
    counts = csum[-1]
    padded = ((counts + tm - 1) // tm) * tm
    ends = jnp.cumsum(padded)
    starts = ends - padded
    cstart = jnp.cumsum(counts) - counts
    pos = jnp.sum(onehot * starts[None, :], axis=1) + rank
    n_tiles = -(-2 * n // tm) + n_exp
    n_slots = n_tiles * tm
    order = jnp.argsort(flat_e, stable=True).astype(jnp.int32)
    tile_start = jnp.arange(n_tiles, dtype=jnp.int32) * tm
    tile_e = jnp.sum((tile_start[:, None] >= ends[None, :]).astype(jnp.int32), axis=1)
    tile_valid = (tile_e < n_exp).astype(jnp.int32)
    tile_e = jnp.minimum(tile_e, n_exp - 1)
    slot = jnp.arange(n_slots, dtype=jnp.int32)
    slot_e = jnp.repeat(tile_e, tm)
    eh = (slot_e[:, None] == jnp.arange(n_exp, dtype=jnp.int32)[None, :]).astype(jnp.int32)
    within = slot - jnp.sum(eh * starts[None, :], axis=1)
    live = (within < jnp.sum(eh * counts[None, :], axis=1)) & (jnp.repeat(tile_valid, tm) > 0)
    src = jnp.where(live, within + jnp.sum(eh * cstart[None, :], axis=1), slot % (2 * n))
    src_flat = jnp.take(order, src)
    src_token = src_flat % n
    slot_gate = jnp.where(live, jnp.take(top_g.reshape(-1), src_flat), 0.0)
    return src_token, slot_gate[:, None], tile_e, tile_valid, pos[:n], pos[n:]


def kernel(x_prompt, x_sample, mem_prompt, cache_ckv, cache_krope, cache_mem_k, cache_mem_v, state_ssm_re, state_ssm_im, page_table, w_in, g_q, w_uq, g_kv, w_uk, w_uv, ssm_lambda_re, ssm_lambda_im, ssm_log_dt, ssm_b_re, ssm_b_im, ssm_c_re, ssm_c_im, ssm_d, w_glu, g_ssm_out, g_mla_out, w_out, ln1_g, ln1_b, w_mq, w_mk, w_mv, w_mo, ln2_g, ln2_b, w_ff_gate, w_ff_up, w_ff_down, w_router, w_e_gate, w_e_up, w_e_down, ln3_g, ln3_b):
    bp_, tp, d = x_prompt.shape
    bs, ts, _ = x_sample.shape
    depth = w_in.shape[0]
    ssm_w = ssm_d.shape[1]
    groups, pstate = ssm_lambda_re.shape[1:]
    gch = ssm_b_re.shape[-1]
    ql = g_q.shape[1]
    kvl = g_kv.shape[1]
    heads = w_uq.shape[2]
    nope = w_uk.shape[3]
    rope = w_uq.shape[3] - nope
    vdim = w_uv.shape[3]
    mlen = mem_prompt.shape[1]
    mh, md = w_mq.shape[2:]
    n_exp = w_router.shape[2]
    n_pages, page = page_table.shape[1], cache_ckv.shape[2]
    past = n_pages * page
    n_p, n_s = bp_ * tp, bs * ts
    n = n_p + n_s
    assert 2 * rope == LANE and groups % SLAB_GROUPS == 0
    assert SLAB_GROUPS * gch == LANE and bp_ <= SUBLANE // 2
    alpha = (2.0 * depth) ** 0.25
    att_scale = (nope + rope) ** -0.5
    mem_scale = md ** -0.5
    qw = kvl + LANE
    half = rope // 2

    o3 = ssm_w + ql + kvl
    w_in_ext = jnp.concatenate([w_in, w_in[:, :, o3 + half:o3 + rope], w_in[:, :, o3:o3 + half]],
                               axis=2).astype(BF16)
    uq_n, uq_r = w_uq[..., :nope], w_uq[..., nope:]
    w_uq_ext = jnp.concatenate([uq_n, uq_r, uq_r[..., half:], uq_r[..., :half]], axis=-1)
    w_uq_ext = w_uq_ext.reshape(depth, ql, heads * (nope + LANE)).astype(BF16)
    w_ukt = jnp.transpose(w_uk, (0, 2, 3, 1)).astype(BF16)
    w_uv_h = jnp.transpose(w_uv, (0, 2, 1, 3)).astype(BF16)
    w_glu_b = w_glu.astype(BF16)
    w_out_b = w_out.astype(BF16)
    w_mq_b = w_mq.reshape(depth, d, mh * md).astype(BF16)
    w_mk_b = w_mk.reshape(depth, d, mh * md).astype(BF16)
    w_mv_b = w_mv.reshape(depth, d, mh * md).astype(BF16)
    w_mo_b = w_mo.reshape(depth, mh * md, d).astype(BF16)
    row = lambda a: a[:, None, :]
    g_q_r, g_kv_r, d_r = row(g_q), row(g_kv), row(ssm_d)
    g_ssm_r, g_mla_r = row(g_ssm_out), row(g_mla_out)
    ln1g, ln1b, ln2g, ln2b, ln3g, ln3b = (row(a) for a in (ln1_g, ln1_b, ln2_g, ln2_b, ln3_g, ln3_b))

    tf_dense = _tile(w_ff_gate.shape[2], 512, mult=LANE)
    dff = w_ff_gate.shape[2]
    ffg, ffu, ffd = _cast_pad_cols(w_ff_gate, dff), _cast_pad_cols(w_ff_up, dff), _cast_pad_rows(w_ff_down, dff)
    eff = w_e_gate.shape[3]
    tf_moe = min(1024, -(-eff // LANE) * LANE)
    effp = -(-eff // tf_moe) * tf_moe
    n_moe = w_e_gate.shape[0]
    eg = _cast_pad_cols(w_e_gate.reshape(n_moe * n_exp, d, eff), effp)
    eu = _cast_pad_cols(w_e_up.reshape(n_moe * n_exp, d, eff), effp)
    ed = _cast_pad_rows(w_e_down.reshape(n_moe * n_exp, eff, d), effp)
    er = -(-n_exp // 16) * 16
    wr_t = jnp.pad(jnp.transpose(w_router, (0, 2, 1)), ((0, 0), (0, er - n_exp), (0, 0)))
    wr_hi = wr_t.astype(BF16)
    wr_lo = (wr_t - wr_hi.astype(F32)).astype(BF16)

    a_re, a_im, bb_re, bb_im = _s5_prep(ssm_lambda_re[:, :, None, :], ssm_lambda_im[:, :, None, :],
                                        ssm_log_dt[:, :, None, None],
                                        jnp.swapaxes(ssm_b_re, 2, 3), jnp.swapaxes(ssm_b_im, 2, 3))
    nslab = groups // SLAB_GROUPS
    eye = jnp.eye(SLAB_GROUPS, dtype=F32)

    def b_blocks(bb):
        bb = bb.reshape(depth, nslab, SLAB_GROUPS, gch, pstate)
        return jnp.einsum('lsgcp,gh->lsgchp', bb, eye).reshape(
            depth, nslab, SLAB_GROUPS * gch, SLAB_GROUPS * pstate).astype(BF16)

    def c_blocks(c):
        c = c.reshape(depth, nslab, SLAB_GROUPS, gch, pstate)
        return jnp.einsum('lsgcp,gh->lshpgc', c, eye).reshape(
            depth, nslab, SLAB_GROUPS * pstate, SLAB_GROUPS * gch).astype(BF16)

    bre_blk, bim_blk = b_blocks(bb_re), b_blocks(bb_im)
    cre_blk, cim_blk = c_blocks(ssm_c_re), c_blocks(ssm_c_im)
    a_re = a_re.reshape(depth, 1, groups * pstate)
    a_im = a_im.reshape(depth, 1, groups * pstate)
    s5_blk = (jnp.concatenate([bre_blk, bim_blk], axis=2), jnp.concatenate([cre_blk, -cim_blk], axis=3),
              a_re, a_im)
    nbp = SUBLANE // 2
    a1_rows = jnp.broadcast_to(a_re, (depth, 2 * nbp, groups * pstate))
    a2_rows = jnp.concatenate([jnp.broadcast_to(-a_im, (depth, nbp, groups * pstate)),
                               jnp.broadcast_to(a_im, (depth, nbp, groups * pstate))], axis=1)
    s5_tok_blk = (bre_blk, bim_blk, cre_blk, cim_blk, a1_rows, a2_rows)

    pos = jnp.concatenate([jnp.tile(jnp.arange(tp, dtype=F32), bp_),
                           jnp.tile(past + jnp.arange(ts, dtype=F32), bs)])[:, None]
    cs = _rope_table(pos, rope)
    memk, memv = _memkv(mem_prompt.reshape(bp_ * mlen, d), w_mk_b, w_mv_b)
    cache_k = cache_mem_k.reshape(depth, bs, mlen * mh, md)
    cache_v = cache_mem_v.reshape(depth, bs, mlen * mh, md)
    cache_krope_t = jnp.swapaxes(cache_krope, 2, 3)
    kn_rows = LANE
    tm_tok = _tile(math.gcd(tp, n_s), 512)

    def scan_rows(u_bt):
        u_t = jnp.swapaxes(u_bt, 0, 1).astype(BF16)
        zz = jnp.zeros_like(u_t)
        rows = jnp.stack([jnp.concatenate([u_t, zz], -1), jnp.concatenate([zz, u_t], -1)], axis=1)
        return rows.reshape(-1, 2 * ssm_w)

    tm_ffn = _tile(n, 512)
    dense_te = lambda i: jnp.full((n // tm_ffn,), i, jnp.int32)
    dense_tv = jnp.ones((n // tm_ffn,), jnp.int32)

    outs = {k: [] for k in ("p_ckv", "p_kr", "p_sr", "p_si", "s_ckv", "s_kr", "s_sr", "s_si")}
    for l in range(depth):
        if l == 0:
            x, u, cq, kcat, ckv, kr, ckv_t = _inproj(
                (x_prompt.reshape(n_p, d), x_sample.reshape(n_s, d)), w_in_ext, l, g_q_r, g_kv_r, cs,
                ssm_w=ssm_w, ql=ql, kvl=kvl, rope=rope, tm=max(LANE, tm_tok // 2))
        else:
            x, u, cq, kcat, ckv, kr, ckv_t = _inproj(x2, w_in_ext, l, g_q_r, g_kv_r, cs, ssm_w=ssm_w, ql=ql,
                                                     kvl=kvl, rope=rope, tm=max(LANE, tm_tok // 2),
                                                     pre=(ffn_out, ln3g, ln3b, alpha))
        qcat = _qproj(cq, w_uq_ext, w_ukt, l, cs, heads=heads, nope=nope, kvl=kvl,
                      scale=att_scale * math.log2(math.e))

        o_p = _flash(qcat, kcat, ckv_t, batch=bp_, seq=tp, heads=heads, kvl=kvl)
        q_s = qcat[n_p:].reshape(bs, ts, heads, qw).transpose(0, 2, 1, 3).reshape(bs, heads * ts, qw)
        k_new = jnp.pad(kcat[n_p:].reshape(bs, ts, qw), ((0, 0), (0, kn_rows - ts), (0, 0)))
        o_s = _paged(page_table, q_s, k_new, cache_ckv, cache_krope_t, l, ts=ts, kvl=kvl, rope=rope)
        o_s = o_s.reshape(bs, heads, ts, kvl).transpose(0, 2, 1, 3).reshape(n_s, heads * kvl)

        y_p, f_p = _s5_scan_tok(u, s5_tok_blk, l, batch=bp_, seq=tp, nb=nbp)
        s0_s = jnp.concatenate([state_ssm_re[l].reshape(bs, -1), state_ssm_im[l].reshape(bs, -1)], axis=0)
        y2_s, f_s = _s5_scan(scan_rows(u[n_p:].reshape(bs, ts, ssm_w)), bs, s5_blk, l, s0_s)
        y2_s = y2_s.reshape(ts, 2, bs, ssm_w).transpose(1, 2, 0, 3).reshape(2, n_s, ssm_w)
        ssm_n = _s5_post(y_p.reshape(n_p, ssm_w), y2_s, u, d_r, w_glu_b, g_ssm_r, l, tm=tm_tok)

        x1, qm = _mixout(o_p, o_s, ssm_n, x, w_uv_h, g_mla_r, w_out_b, ln1g, ln1b, w_mq_b, l,
                         heads=heads, kvl=kvl, alpha=alpha, mscale=mem_scale, tm=tm_tok)

        c_p = _cross_prompt(qm, memk, memv, l, batch=bp_, seq=tp, mh=mh, md=md)
        c_s = _cross_sample(qm[n_p:], cache_k, cache_v, l, bs=bs, ts=ts, mh=mh, md=md)

        if l % 2 == 0:
            x2, = _crossout(c_p, c_s, x1, w_mo_b, ln2g, ln2b, l, alpha=alpha, tm=tm_tok)
            ffn_out = [_ffn(x2, ffg, ffu, ffd, dense_te(l // 2), dense_tv, tm_ffn, tf_dense)]
        else:
            x2, top_i, top_g = _crossout(c_p, c_s, x1, w_mo_b, ln2g, ln2b, l, alpha=alpha, tm=tm_tok,
                                         router=(wr_hi, wr_lo, l // 2, n_exp))
            tm_moe = 512
            src, gate, tile_e, tile_v, pos1, pos2 = _moe_plan(top_i, top_g, n_exp, tm_moe)
            take = lambda a, idx: a.at[idx].get(mode="promise_in_bounds")
            ys = _ffn(take(x2, src), eg, eu, ed, tile_e + (l // 2) * n_exp, tile_v, tm_moe, tf_moe,
                      scale=gate)
            ffn_out = [take(ys, pos1), take(ys, pos2)]

        outs["p_ckv"].append(ckv[:n_p].reshape(bp_, tp, kvl))
        outs["p_kr"].append(kr[:n_p].reshape(bp_, tp, rope))
        outs["p_sr"].append(f_p[:bp_].reshape(bp_, groups, pstate))
        outs["p_si"].append(f_p[nbp:nbp + bp_].reshape(bp_, groups, pstate))
        outs["s_ckv"].append(ckv[n_p:].reshape(bs, ts, kvl))
        outs["s_kr"].append(kr[n_p:].reshape(bs, ts, rope))
        outs["s_sr"].append(f_s[:bs].reshape(bs, groups, pstate))
        outs["s_si"].append(f_s[bs:].reshape(bs, groups, pstate))

    y_p, y_s = _addln(x2, ffn_out, ln3g, ln3b, depth - 1, alpha=alpha, tm=tm_tok, n_p=n_p)
    st = {k: jnp.stack(v) for k, v in outs.items()}
    return (y_p.reshape(bp_, tp, d), y_s.reshape(bs, ts, d),
            st["p_ckv"], st["p_kr"], st["p_sr"], st["p_si"],
            memk.reshape(depth, bp_, mlen, mh, md), memv.reshape(depth, bp_, mlen, mh, md),
            st["s_ckv"], st["s_kr"], st["s_sr"], st["s_si"])
```

```python
import functools
import math

import jax
import jax.numpy as jnp
from jax import lax
from jax.experimental import pallas as pl
from jax.experimental.pallas import tpu as pltpu

F32 = jnp.float32
BF16 = jnp.bfloat16

LANE = 128
SUBLANE = 8
VMEM_LIMIT = 56 * 2 ** 20
LN_EPS = 1e-5
RMS_EPS = 1e-6
ROPE_BASE = 10000.0
SLAB_GROUPS = 8
NEG = float(jnp.finfo(jnp.float32).min)


def _dot(a, b):
    return jnp.dot(a, b, preferred_element_type=F32)


def _dot_nt(a, b):
    return lax.dot_general(a, b, (((1,), (1,)), ((), ())), preferred_element_type=F32)


def _ln(v, g, b):
    vc = v - jnp.mean(v, -1, keepdims=True)
    var = jnp.mean(vc * vc, -1, keepdims=True)
    return vc * lax.rsqrt(var + LN_EPS) * g + b


def _rms(v, g):
    return v * lax.rsqrt(jnp.mean(v * v, -1, keepdims=True) + RMS_EPS) * g


def _tile(n, pref, mult=SUBLANE):
    best = None
    for t in range(mult, min(n, pref) + 1, mult):
        if n % t == 0:
            best = t
    assert best is not None, (n, pref, mult)
    return best


def _params(*sem):
    return pltpu.CompilerParams(dimension_semantics=sem, vmem_limit_bytes=VMEM_LIMIT)


def _rows(tm, w):
    return pl.BlockSpec((tm, w), lambda i: (i, 0))


def _const(shape):
    return pl.BlockSpec(shape, lambda i: (0,) * len(shape))


def _layer(l, shape):
    return pl.BlockSpec((None,) + shape, lambda i: (l,) + (0,) * len(shape))


def _rope_table_body(pos_ref, inv_ref, cs_ref):
    ang = pos_ref[...] * inv_ref[...]
    lane = lax.broadcasted_iota(jnp.int32, ang.shape, 1)
    c = jnp.cos(ang)
    s = jnp.sin(ang)
    cs_ref[...] = jnp.where(lane < LANE // 2, c, jnp.where(lane < 3 * LANE // 4, -s, s))


def _rope_table(pos, rope):
    n = pos.shape[0]
    half = rope // 2
    inv = ROPE_BASE ** (-jnp.arange(half, dtype=F32) / half)
    inv4 = jnp.tile(inv, 4)[None, :]
    tm = _tile(n, 1024)
    return pl.pallas_call(
        _rope_table_body, grid=(n // tm,),
        in_specs=[_rows(tm, 1), _const((1, LANE))], out_specs=_rows(tm, LANE),
        out_shape=jax.ShapeDtypeStruct((n, LANE), F32), compiler_params=_params("parallel"),
        name="rope_table")(pos, inv4)


def _inproj_body(*refs, ssm_w, ql, kvl, rope, n_y, alpha, npt):
    u_ref, cq_ref, kcat_ref, ckv_ref, kr_ref, ckvt_ref = refs[-6:]
    if npt:
        xp_ref, xs_ref, w_ref, gq_ref, gkv_ref, cs_ref, xo_ref = refs[:-6]
        x = jnp.where(pl.program_id(0) < npt, xp_ref[...], xs_ref[...])
        xo_ref[...] = x
    elif n_y:
        x_ref = refs[0]
        g_ref, b_ref, w_ref, gq_ref, gkv_ref, cs_ref, xo_ref = refs[1 + n_y:-6]
        y = refs[1][...]
        for r in refs[2:1 + n_y]:
            y = y + r[...]
        x = _ln(alpha * x_ref[...] + y, g_ref[...], b_ref[...])
        xo_ref[...] = x
    else:
        x_ref, w_ref, gq_ref, gkv_ref, cs_ref = refs[:-6]
        x = x_ref[...]
    xb = x.astype(BF16)
    o1, o2, o3 = ssm_w, ssm_w + ql, ssm_w + ql + kvl
    u_ref[...] = _dot(xb, w_ref[:, :o1])
    cq_ref[...] = _rms(_dot(xb, w_ref[:, o1:o2]), gq_ref[...]).astype(BF16)
    ckv = _rms(_dot(xb, w_ref[:, o2:o3]), gkv_ref[...])
    ckv_ref[...] = ckv
    ckvt_ref[...] = jnp.transpose(ckv).astype(BF16)
    kr2 = _dot(xb, w_ref[:, o3:o3 + LANE]) * cs_ref[...]
    kr = kr2 + pltpu.roll(kr2, LANE // 2, axis=1)
    kr_ref[...] = kr[:, :rope]
    kcat_ref[:, :kvl] = ckv.astype(BF16)
    kcat_ref[:, kvl:] = kr.astype(BF16)


def _inproj(x, w_ext, l, gq, gkv, cs, *, ssm_w, ql, kvl, rope, tm, pre=None):
    split = isinstance(x, tuple)
    assert not (split and pre is not None)
    n = x[0].shape[0] + x[1].shape[0] if split else x.shape[0]
    d = w_ext.shape[1]
    cols = w_ext.shape[2]
    assert tm % LANE == 0
    ys, alpha = ([], 0.0) if pre is None else (list(pre[0]), pre[3])
    npt = x[0].shape[0] // tm if split else 0
    body = functools.partial(_inproj_body, ssm_w=ssm_w, ql=ql, kvl=kvl, rope=rope, n_y=len(ys), alpha=alpha,
                             npt=npt)
    if split:
        in_specs = list(_split_rows(tm, d, npt))
        args = list(x)
    else:
        in_specs = [_rows(tm, d)] * (1 + len(ys))
        args = [x, *ys]
    out_specs, out_shape = [], []
    if split:
        out_specs.append(_rows(tm, d))
        out_shape.append(jax.ShapeDtypeStruct((n, d), F32))
    if pre is not None:
        in_specs += [_layer(l - 1, (1, d)), _layer(l - 1, (1, d))]
        args += [pre[1], pre[2]]
        out_specs.append(_rows(tm, d))
        out_shape.append(jax.ShapeDtypeStruct((n, d), F32))
    in_specs += [_layer(l, (d, cols)), _layer(l, (1, ql)), _layer(l, (1, kvl)), _rows(tm, LANE)]
    args += [w_ext, gq, gkv, cs]
    out_specs += [_rows(tm, ssm_w), _rows(tm, ql), _rows(tm, kvl + LANE), _rows(tm, kvl),
                  _rows(tm, rope), pl.BlockSpec((kvl, tm), lambda i: (0, i))]
    out_shape += [jax.ShapeDtypeStruct((n, ssm_w), F32), jax.ShapeDtypeStruct((n, ql), BF16),
                  jax.ShapeDtypeStruct((n, kvl + LANE), BF16), jax.ShapeDtypeStruct((n, kvl), F32),
                  jax.ShapeDtypeStruct((n, rope), F32), jax.ShapeDtypeStruct((kvl, n), BF16)]
    return pl.pallas_call(
        body, grid=(n // tm,), in_specs=in_specs, out_specs=out_specs, out_shape=out_shape,
        compiler_params=_params("parallel"), name="in_proj")(*args)


def _qproj_body(cq_ref, wuq_ref, wukt_ref, cs_ref, q_ref, *, heads, nope, kvl, scale):
    q = _dot(cq_ref[...], wuq_ref[...])
    cs = cs_ref[...]
    lane = lax.broadcasted_iota(jnp.int32, cs.shape, 1)
    hw = nope + LANE
    qw = kvl + LANE
    for h in range(heads):
        qlat = _dot(q[:, h * hw:h * hw + nope].astype(BF16), wukt_ref[h])
        q2 = q[:, h * hw + nope:(h + 1) * hw] * cs
        qr = q2 + pltpu.roll(q2, LANE // 2, axis=1)
        qr = jnp.where(lane < LANE // 2, qr, 0.0)
        q_ref[:, h * qw:h * qw + kvl] = (qlat * scale).astype(BF16)
        q_ref[:, h * qw + kvl:(h + 1) * qw] = (qr * scale).astype(BF16)


def _qproj(cq, wuq_ext, wukt, l, cs, *, heads, nope, kvl, scale):
    n, ql = cq.shape
    hw = nope + LANE
    qw = kvl + LANE
    tm = _tile(n, 512)
    body = functools.partial(_qproj_body, heads=heads, nope=nope, kvl=kvl, scale=scale)
    return pl.pallas_call(
        body, grid=(n // tm,),
        in_specs=[_rows(tm, ql), _layer(l, (ql, heads * hw)), _layer(l, (heads, nope, kvl)),
                  _rows(tm, LANE)],
        out_specs=_rows(tm, heads * qw),
        out_shape=jax.ShapeDtypeStruct((n, heads * qw), BF16),
        compiler_params=_params("parallel"), name="q_proj")(cq, wuq_ext, wukt, cs)


def _flash_body(q_ref, k_ref, vt_ref, o_ref, m_ref, l_ref, acc_ref, *, heads, kvl, qw):
    i = pl.program_id(1)
    j = pl.program_id(2)

    @pl.when(j == 0)
    def _():
        m_ref[...] = jnp.full(m_ref.shape, NEG, F32)
        l_ref[...] = jnp.zeros(l_ref.shape, F32)
        acc_ref[...] = jnp.zeros(acc_ref.shape, F32)

    def step(masked):
        k = k_ref[...]
        vt = vt_ref[...]
        for h in range(heads):
            s = _dot_nt(k, q_ref[:, h * qw:(h + 1) * qw])
            if masked:
                key = lax.broadcasted_iota(jnp.int32, s.shape, 0)
                qry = lax.broadcasted_iota(jnp.int32, s.shape, 1)
                s = jnp.where(key <= qry, s, NEG)
            m_prev = m_ref[h]
            m_new = jnp.maximum(m_prev, jnp.max(s, axis=0, keepdims=True))
            a = jnp.exp2(m_prev - m_new)
            p = jnp.exp2(s - m_new)
            l_ref[h] = a * l_ref[h] + jnp.sum(p, axis=0, keepdims=True)
            acc_ref[h] = a * acc_ref[h] + _dot(vt, p.astype(BF16))
            m_ref[h] = m_new

    @pl.when(j < i)
    def _():
        step(False)

    @pl.when(j == i)
    def _():
        step(True)
        for h in range(heads):
            o_ref[:, h * kvl:(h + 1) * kvl] = jnp.transpose(acc_ref[h] / l_ref[h]).astype(BF16)


def _flash(qcat, kcat, ckv_t, *, batch, seq, heads, kvl):
    qw = kvl + LANE
    tq = _tile(seq, 512, mult=LANE)
    nq = seq // tq
    body = functools.partial(_flash_body, heads=heads, kvl=kvl, qw=qw)
    return pl.pallas_call(
        body, grid=(batch, nq, nq),
        in_specs=[pl.BlockSpec((tq, heads * qw), lambda b, i, j: (b * nq + i, 0)),
                  pl.BlockSpec((tq, qw), lambda b, i, j: (b * nq + jnp.minimum(i, j), 0)),
                  pl.BlockSpec((kvl, tq), lambda b, i, j: (0, b * nq + jnp.minimum(i, j)))],
        out_specs=pl.BlockSpec((tq, heads * kvl), lambda b, i, j: (b * nq + i, 0)),
        out_shape=jax.ShapeDtypeStruct((batch * seq, heads * kvl), BF16),
        scratch_shapes=[pltpu.VMEM((heads, 1, tq), F32), pltpu.VMEM((heads, 1, tq), F32),
                        pltpu.VMEM((heads, kvl, tq), F32)],
        compiler_params=_params("parallel", "parallel", "arbitrary"), name="prompt_attn")(
            qcat, kcat, ckv_t)


def _paged_body(pt_ref, q_ref, kn_ref, ckv_hbm, krt_hbm, o_ref, ckv_buf, krt_buf, sem, s_ref, kb_ref,
                *, layer, nseq, n_pages, page, kvl, rope, ts, chunk_pages):
    b = pl.program_id(0)
    nb = pl.num_programs(0)
    slot = lax.rem(b, 2)
    step_pages = nseq * n_pages

    def fetch(bb, sl):
        def issue(i, c):
            for k in range(2):
                p = 2 * i + k
                pg = pt_ref[bb * step_pages + p]
                pltpu.make_async_copy(ckv_hbm.at[layer, pg], ckv_buf.at[sl, p], sem.at[sl, 0]).start(k)
                pltpu.make_async_copy(krt_hbm.at[layer, pg], krt_buf.at[sl, p], sem.at[sl, 1]).start(1 - k)
            return c
        lax.fori_loop(0, step_pages // 2, issue, 0, unroll=4)

    @pl.when(b == 0)
    def _():
        fetch(0, 0)

    @pl.when(b + 1 < nb)
    def _():
        fetch(b + 1, 1 - slot)

    pltpu.make_async_copy(ckv_buf.at[slot], ckv_buf.at[slot], sem.at[slot, 0]).wait()
    pltpu.make_async_copy(krt_buf.at[slot], krt_buf.at[slot], sem.at[slot, 1]).wait()

    ck = chunk_pages * page
    n_chunks = n_pages // chunk_pages
    for sq in range(nseq):
        q = q_ref[sq]
        ql = q[:, :kvl]
        qr = q[:, kvl:kvl + rope]
        for c in range(n_chunks):
            p0 = sq * n_pages + c * chunk_pages
            kc = ckv_buf[slot, pl.ds(p0, chunk_pages)].reshape(ck, kvl).astype(BF16)
            rt = jnp.concatenate([krt_buf[slot, p0 + p] for p in range(chunk_pages)], axis=1).astype(BF16)
            s_ref[sq, :, c * ck:(c + 1) * ck] = _dot_nt(ql, kc) + _dot(qr, rt)
            kb_ref[sq, c * ck:(c + 1) * ck, :] = kc

        kn = kn_ref[sq]
        sn = _dot_nt(q, kn)
        qj = lax.broadcasted_iota(jnp.int32, sn.shape, 0) & (ts - 1)
        col = lax.broadcasted_iota(jnp.int32, sn.shape, 1)
        sn = jnp.where(col <= qj, sn, NEG)

        m = jnp.maximum(jnp.max(s_ref[sq], axis=1, keepdims=True), jnp.max(sn, axis=1, keepdims=True))
        pn = jnp.exp2(sn - m)
        l = jnp.sum(pn, axis=1, keepdims=True)
        acc = _dot(pn.astype(BF16), kn[:, :kvl])
        for c in range(n_chunks):
            p = jnp.exp2(s_ref[sq, :, c * ck:(c + 1) * ck] - m)
            l = l + jnp.sum(p, axis=1, keepdims=True)
            acc = acc + _dot(p.astype(BF16), kb_ref[sq, c * ck:(c + 1) * ck, :])
        o_ref[sq] = (acc / l).astype(BF16)


def _paged(page_table, q_s, k_new, cache_ckv, cache_krope_t, l, *, ts, kvl, rope):
    bs, rows, qw = q_s.shape
    n_pages = page_table.shape[1]
    page = cache_ckv.shape[2]
    assert cache_krope_t.shape[2:] == (rope, page) and n_pages % 2 == 0
    chunk_pages = _tile(n_pages, 16, mult=1)
    past = n_pages * page
    assert ts & (ts - 1) == 0
    nseq = 2 if bs % 2 == 0 else 1
    body = functools.partial(_paged_body, layer=l, nseq=nseq, n_pages=n_pages, page=page, kvl=kvl,
                             rope=rope, ts=ts, chunk_pages=chunk_pages)
    grid_spec = pltpu.PrefetchScalarGridSpec(
        num_scalar_prefetch=1, grid=(bs // nseq,),
        in_specs=[pl.BlockSpec((nseq, rows, qw), lambda b, pt: (b, 0, 0)),
                  pl.BlockSpec((nseq, k_new.shape[1], qw), lambda b, pt: (b, 0, 0)),
                  pl.BlockSpec(memory_space=pl.ANY), pl.BlockSpec(memory_space=pl.ANY)],
        out_specs=pl.BlockSpec((nseq, rows, kvl), lambda b, pt: (b, 0, 0)),
        scratch_shapes=[pltpu.VMEM((2, nseq * n_pages, page, kvl), F32),
                        pltpu.VMEM((2, nseq * n_pages, rope, page), F32),
                        pltpu.SemaphoreType.DMA((2, 2)),
                        pltpu.VMEM((nseq, rows, past), F32),
                        pltpu.VMEM((nseq, past, kvl), BF16)])
    return pl.pallas_call(
        body, grid_spec=grid_spec, out_shape=jax.ShapeDtypeStruct((bs, rows, kvl), BF16),
        compiler_params=_params("arbitrary"), name="paged_attn")(
            page_table.reshape(-1), q_s, k_new, cache_ckv, cache_krope_t)


def _s5_prep_body(lr_ref, li_ref, ldt_ref, br_ref, bi_ref, ar_ref, ai_ref, bbr_ref, bbi_ref):
    lr = lr_ref[...]
    li = li_ref[...]
    dt = jnp.exp(ldt_ref[...])
    mag = jnp.exp(lr * dt)
    ar = mag * jnp.cos(li * dt)
    ai = mag * jnp.sin(li * dt)
    den = lr * lr + li * li
    zr = ((ar - 1.0) * lr + ai * li) / den
    zi = (ai * lr - (ar - 1.0) * li) / den
    ar_ref[...] = ar
    ai_ref[...] = ai
    br = br_ref[...]
    bi = bi_ref[...]
    bbr_ref[...] = zr * br - zi * bi
    bbi_ref[...] = zr * bi + zi * br


def _s5_prep(lam_re, lam_im, log_dt, b_re_t, b_im_t):
    nl, g, _, p = lam_re.shape
    c = b_re_t.shape[2]
    gp = pl.BlockSpec((None, g, 1, p), lambda l: (l, 0, 0, 0))
    gcp = pl.BlockSpec((None, g, c, p), lambda l: (l, 0, 0, 0))
    return pl.pallas_call(
        _s5_prep_body, grid=(nl,),
        in_specs=[gp, gp, pl.BlockSpec((None, g, 1, 1), lambda l: (l, 0, 0, 0)), gcp, gcp],
        out_specs=[gp, gp, gcp, gcp],
        out_shape=[jax.ShapeDtypeStruct((nl, g, 1, p), F32)] * 2
        + [jax.ShapeDtypeStruct((nl, g, c, p), F32)] * 2,
        compiler_params=_params("parallel"), name="s5_prep")(lam_re, lam_im, log_dt, b_re_t, b_im_t)


def _swap_halves(z, nb):
    if 2 * nb == SUBLANE:
        return pltpu.roll(z, nb, axis=0)
    return jnp.concatenate([z[nb:], z[:nb]], axis=0)


def _s5_scan_body(u_ref, bw_ref, cw_ref, ar_ref, ai_ref, s0_ref, y_ref, f_ref, x_ref, st_ref,
                  *, nb, tc, nslab, sc, sp, lw):
    step_rows = 2 * nb
    w = nslab * sc

    @pl.when(pl.program_id(0) == 0)
    def _():
        st_ref[...] = s0_ref[...]

    for j in range(nslab):
        lhs = jnp.concatenate([u_ref[:, j * sc:(j + 1) * sc], u_ref[:, w + j * sc:w + (j + 1) * sc]], axis=1)
        x_ref[:, j * sp:(j + 1) * sp] = _dot(lhs, bw_ref[j])

    im_row = lax.broadcasted_iota(jnp.int32, (step_rows, lw), 0) >= nb
    for p0 in range(0, nslab * sp, lw):
        lanes = slice(p0, p0 + lw)
        a1 = jnp.broadcast_to(ar_ref[:, lanes], (step_rows, lw))
        ai = jnp.broadcast_to(ai_ref[:, lanes], (step_rows, lw))
        a2 = jnp.where(im_row, ai, -ai)
        z = st_ref[:, lanes]
        for t in range(tc):
            rows = slice(t * step_rows, (t + 1) * step_rows)
            z = a1 * z + a2 * _swap_halves(z, nb) + x_ref[rows, lanes]
            x_ref[rows, lanes] = z
        st_ref[:, lanes] = z

    re_row = (lax.broadcasted_iota(jnp.int32, (tc * step_rows, sc), 0) & nb) == 0
    for j in range(nslab):
        y2 = _dot(x_ref[:, j * sp:(j + 1) * sp].astype(BF16), cw_ref[j])
        y_ref[:, j * sc:(j + 1) * sc] = jnp.where(re_row, y2[:, :sc], y2[:, sc:])
    f_ref[...] = st_ref[...]


def _s5_scan(u2, nb, blk, l, s0):
    bw, cw, ar, ai = blk
    rows, w2 = u2.shape
    nslab, sc2, sp = bw.shape[1:]
    sc = sc2 // 2
    w = w2 // 2
    gp = nslab * sp
    step_rows = 2 * nb
    assert nb & (nb - 1) == 0 and step_rows % SUBLANE == 0
    steps = rows // step_rows
    tc = _tile(steps, max(1, 256 // step_rows), mult=1)
    rc = tc * step_rows
    lw = _tile(gp, max(LANE, (8 * SUBLANE * LANE) // step_rows), mult=LANE)
    body = functools.partial(_s5_scan_body, nb=nb, tc=tc, nslab=nslab, sc=sc, sp=sp, lw=lw)
    return pl.pallas_call(
        body, grid=(steps // tc,),
        in_specs=[_rows(rc, w2), _layer(l, (nslab, sc2, sp)), _layer(l, (nslab, sp, sc2)),
                  _layer(l, (1, gp)), _layer(l, (1, gp)), _const((step_rows, gp))],
        out_specs=[_rows(rc, w), _const((step_rows, gp))],
        out_shape=[jax.ShapeDtypeStruct((rows, w), F32), jax.ShapeDtypeStruct((step_rows, gp), F32)],
        scratch_shapes=[pltpu.VMEM((rc, gp), F32), pltpu.VMEM((step_rows, gp), F32)],
        compiler_params=_params("arbitrary"), name="s5_scan")(u2, bw, cw, ar, ai, s0)


def _s5_scan_tok_body(*refs, nreal, nb, tc, pitch, nslab, sc, sp, lw):
    u_refs = refs[:nreal]
    (bre_ref, bim_ref, cre_ref, cim_ref, a1_ref, a2_ref, y_ref, f_ref, ub_ref, x_ref, st_ref) = refs[nreal:]
    step_rows = 2 * nb
    half = nb * pitch
    nq = sp // LANE

    @pl.when(pl.program_id(0) == 0)
    def _():
        st_ref[...] = jnp.zeros(st_ref.shape, F32)
        ub_ref[...] = jnp.zeros(ub_ref.shape, F32)

    for b in range(nreal):
        ub_ref[b * pitch:b * pitch + tc, :] = u_refs[b][...]
    for j in range(nslab):
        uj = ub_ref[:, j * sc:(j + 1) * sc].astype(BF16)
        xr = _dot(uj, bre_ref[j])
        xi = _dot(uj, bim_ref[j])
        for q in range(nq):
            x_ref[j * nq + q, 0:half, :] = xr[:, q * LANE:(q + 1) * LANE]
            x_ref[j * nq + q, half:2 * half, :] = xi[:, q * LANE:(q + 1) * LANE]

    npass = nslab * sp // lw
    nl = lw // LANE
    zs = [st_ref[:, p * lw:(p + 1) * lw] for p in range(npass)]
    for t in range(tc):
        rows = pl.ds(t, step_rows, stride=pitch)
        for p in range(npass):
            lanes = slice(p * lw, (p + 1) * lw)
            x_t = jnp.concatenate([x_ref[p * nl + q, rows, :] for q in range(nl)], axis=1)
            z = a1_ref[:, lanes] * zs[p] + a2_ref[:, lanes] * _swap_halves(zs[p], nb) + x_t
            for q in range(nl):
                x_ref[p * nl + q, rows, :] = z[:, q * LANE:(q + 1) * LANE]
            zs[p] = z
    for p in range(npass):
        st_ref[:, p * lw:(p + 1) * lw] = zs[p]

    for j in range(nslab):
        sr = jnp.concatenate([x_ref[j * nq + q, 0:half, :] for q in range(nq)], axis=1).astype(BF16)
        si = jnp.concatenate([x_ref[j * nq + q, half:2 * half, :] for q in range(nq)], axis=1).astype(BF16)
        y = _dot(sr, cre_ref[j]) - _dot(si, cim_ref[j])
        for b in range(nreal):
            y_ref[b, :, j * sc:(j + 1) * sc] = y[b * pitch:b * pitch + tc]
    f_ref[...] = st_ref[...]


def _s5_scan_tok(u, blk, l, *, batch, seq, nb):
    bre, bim, cre, cim, a1, a2 = blk
    w = u.shape[1]
    nslab, sc, sp = bre.shape[1:]
    gp = nslab * sp
    step_rows = 2 * nb
    assert step_rows == SUBLANE and batch <= nb
    tc = _tile(seq, 64, mult=2 * SUBLANE)
    pitch = tc + SUBLANE // 2
    nt = seq // tc
    lw = _tile(gp, 8 * LANE, mult=LANE)
    body = functools.partial(_s5_scan_tok_body, nreal=batch, nb=nb, tc=tc, pitch=pitch, nslab=nslab,
                             sc=sc, sp=sp, lw=lw)
    u_specs = [pl.BlockSpec((tc, w), functools.partial(lambda i, b: (b * nt + i, 0), b=b))
               for b in range(batch)]
    return pl.pallas_call(
        body, grid=(nt,),
        in_specs=u_specs + [_layer(l, (nslab, sc, sp)), _layer(l, (nslab, sc, sp)),
                            _layer(l, (nslab, sp, sc)), _layer(l, (nslab, sp, sc)),
                            _layer(l, (step_rows, gp)), _layer(l, (step_rows, gp))],
        out_specs=[pl.BlockSpec((batch, tc, w), lambda i: (0, i, 0)), _const((step_rows, gp))],
        out_shape=[jax.ShapeDtypeStruct((batch, seq, w), F32), jax.ShapeDtypeStruct((step_rows, gp), F32)],
        scratch_shapes=[pltpu.VMEM((nb * pitch, w), F32), pltpu.VMEM((gp // LANE, step_rows * pitch, LANE), F32),
                        pltpu.VMEM((step_rows, gp), F32)],
        compiler_params=_params("arbitrary"), name="s5_scan_tok")(
            *([u] * batch), bre, bim, cre, cim, a1, a2)


def _s5_post_body(yp_ref, sre_ref, sim_ref, u_ref, d_ref, wglu_ref, g_ref, o_ref, *, npt):
    y = jnp.where(pl.program_id(0) < npt, yp_ref[...], sre_ref[...] + sim_ref[...])
    yf = y + d_ref[...] * u_ref[...]
    g = jax.nn.gelu(yf)
    z = g * jax.nn.sigmoid(_dot(g.astype(BF16), wglu_ref[...]))
    o_ref[...] = _rms(z, g_ref[...]).astype(BF16)


def _s5_post(y_p, y_s, u, d, wglu, g, l, *, tm):
    n, w = u.shape
    npt = y_p.shape[0] // tm

    def smap(part):
        return pl.BlockSpec((None, tm, w), lambda i: (part, jnp.maximum(i - npt, 0), 0))

    return pl.pallas_call(
        functools.partial(_s5_post_body, npt=npt), grid=(n // tm,),
        in_specs=[_split_rows(tm, w, npt)[0], smap(0), smap(1), _rows(tm, w), _layer(l, (1, w)),
                  _layer(l, (w, w)), _layer(l, (1, w))],
        out_specs=_rows(tm, w), out_shape=jax.ShapeDtypeStruct((n, w), BF16),
        compiler_params=_params("parallel"), name="s5_post")(y_p, y_s, y_s, u, d, wglu, g)


def _split_rows(tm, w, npt):
    return (pl.BlockSpec((tm, w), lambda i: (jnp.minimum(i, npt - 1), 0)),
            pl.BlockSpec((tm, w), lambda i: (jnp.maximum(i - npt, 0), 0)))


def _mixout_body(op_ref, os_ref, ssm_ref, x_ref, wuv_ref, gm_ref, wout_ref, g1_ref, b1_ref, wmq_ref,
                 x1_ref, qm_ref, *, heads, kvl, ssm_w, alpha, mscale, npt):
    is_prompt = pl.program_id(0) < npt

    def o_head(h):
        cols = slice(h * kvl, (h + 1) * kvl)
        return jnp.where(is_prompt, op_ref[:, cols], os_ref[:, cols])

    mla = jnp.concatenate([_dot(o_head(h), wuv_ref[h]) for h in range(heads)], axis=-1)
    mla_n = _rms(mla, gm_ref[...]).astype(BF16)
    m = _dot(ssm_ref[...], wout_ref[:ssm_w, :]) + _dot(mla_n, wout_ref[ssm_w:, :])
    x1 = _ln(alpha * x_ref[...] + m, g1_ref[...], b1_ref[...])
    x1_ref[...] = x1
    qm_ref[...] = (_dot(x1.astype(BF16), wmq_ref[...]) * mscale).astype(BF16)


def _mixout(o_p, o_s, ssm_n, x, wuv, gm, wout, g1, b1, wmq, l, *, heads, kvl, alpha, mscale, tm):
    n, d = x.shape
    ssm_w = ssm_n.shape[1]
    v = wuv.shape[3]
    mix = wout.shape[1]
    mq = wmq.shape[2]
    npt = o_p.shape[0] // tm
    body = functools.partial(_mixout_body, heads=heads, kvl=kvl, ssm_w=ssm_w, alpha=alpha, mscale=mscale,
                             npt=npt)
    return pl.pallas_call(
        body, grid=(n // tm,),
        in_specs=[*_split_rows(tm, heads * kvl, npt), _rows(tm, ssm_w), _rows(tm, d),
                  _layer(l, (heads, kvl, v)), _layer(l, (1, heads * v)), _layer(l, (mix, d)),
                  _layer(l, (1, d)), _layer(l, (1, d)), _layer(l, (d, mq))],
        out_specs=[_rows(tm, d), _rows(tm, mq)],
        out_shape=[jax.ShapeDtypeStruct((n, d), F32), jax.ShapeDtypeStruct((n, mq), BF16)],
        compiler_params=_params("parallel"), name="mix_out")(
            o_p, o_s, ssm_n, x, wuv, gm, wout, g1, b1, wmq)


def _memkv_body(mem_ref, wk_ref, wv_ref, k_ref, v_ref):
    mb = mem_ref[...].astype(BF16)
    k_ref[...] = _dot(mb, wk_ref[...])
    v_ref[...] = _dot(mb, wv_ref[...])


def _memkv(mem, wk, wv):
    nl, d, hk = wk.shape
    rows = mem.shape[0]
    w = pl.BlockSpec((None, d, hk), lambda l: (l, 0, 0))
    o = pl.BlockSpec((None, rows, hk), lambda l: (l, 0, 0))
    return pl.pallas_call(
        _memkv_body, grid=(nl,), in_specs=[_const((rows, d)), w, w], out_specs=[o, o],
        out_shape=[jax.ShapeDtypeStruct((nl, rows, hk), F32)] * 2,
        compiler_params=_params("parallel"), name="mem_kv")(mem, wk, wv)


def _softmax_rows(s):
    p = jnp.exp(s - jnp.max(s, axis=1, keepdims=True))
    return p / jnp.sum(p, axis=1, keepdims=True)


def _cross_p_body(q_ref, mk_ref, mv_ref, o_ref, *, mh, md):
    mk = mk_ref[...].astype(BF16)
    mv = mv_ref[...].astype(BF16)
    for h in range(mh):
        cols = slice(h * md, (h + 1) * md)
        p = _softmax_rows(_dot_nt(q_ref[:, cols], mk[:, cols]))
        o_ref[:, cols] = _dot(p.astype(BF16), mv[:, cols]).astype(BF16)


def _cross_prompt(qm, memk, memv, l, *, batch, seq, mh, md):
    mlen = memk.shape[1] // batch
    hk = mh * md
    tm = _tile(seq, 512)
    nt = seq // tm
    kv = pl.BlockSpec((None, mlen, hk), lambda b, i: (l, b, 0))
    body = functools.partial(_cross_p_body, mh=mh, md=md)
    return pl.pallas_call(
        body, grid=(batch, nt),
        in_specs=[pl.BlockSpec((tm, hk), lambda b, i: (b * nt + i, 0)), kv, kv],
        out_specs=pl.BlockSpec((tm, hk), lambda b, i: (b * nt + i, 0)),
        out_shape=jax.ShapeDtypeStruct((batch * seq, hk), BF16),
        compiler_params=_params("parallel", "parallel"), name="cross_prompt")(qm, memk, memv)


def _cross_s_body(q_ref, mk_ref, mv_ref, o_ref, *, bb, ts, mh, md):
    q_all = q_ref[...].astype(F32)
    rows_kv = mk_ref.shape[1]
    row_head = lax.broadcasted_iota(jnp.int32, (mh * ts, rows_kv), 0) >> (ts.bit_length() - 1)
    col_head = lax.broadcasted_iota(jnp.int32, (mh * ts, rows_kv), 1) & (mh - 1)
    own = row_head == col_head
    outs = []
    for b in range(bb):
        q = q_all[b * ts:(b + 1) * ts, :]
        qh = jnp.concatenate([q[:, h * md:(h + 1) * md] for h in range(mh)], axis=0).astype(BF16)
        s = jnp.where(own, _dot_nt(qh, mk_ref[b].astype(BF16)), NEG)
        o = _dot(_softmax_rows(s).astype(BF16), mv_ref[b].astype(BF16))
        outs.append(jnp.concatenate([o[h * ts:(h + 1) * ts] for h in range(mh)], axis=1))
    o_ref[...] = jnp.concatenate(outs, axis=0).astype(BF16)


def _cross_sample(qm_s, cache_k, cache_v, l, *, bs, ts, mh, md):
    rows_kv = cache_k.shape[2]
    hk = mh * md
    assert mh & (mh - 1) == 0 and ts & (ts - 1) == 0
    bb = _tile(bs, 8, mult=1)
    kv = pl.BlockSpec((None, bb, rows_kv, md), lambda i: (l, i, 0, 0))
    body = functools.partial(_cross_s_body, bb=bb, ts=ts, mh=mh, md=md)
    return pl.pallas_call(
        body, grid=(bs // bb,),
        in_specs=[_rows(bb * ts, hk), kv, kv], out_specs=_rows(bb * ts, hk),
        out_shape=jax.ShapeDtypeStruct((bs * ts, hk), BF16),
        compiler_params=_params("parallel"), name="cross_sample")(qm_s, cache_k, cache_v)


def _top2(lt, n_exp):
    e = jnp.exp(lt - jnp.max(lt, axis=0, keepdims=True))
    probs = e / jnp.sum(e, axis=0, keepdims=True)
    eidx = lax.broadcasted_iota(jnp.int32, probs.shape, 0)
    m1 = jnp.max(probs, axis=0, keepdims=True)
    i1 = jnp.min(jnp.where(probs == m1, eidx, n_exp), axis=0, keepdims=True)
    rest = jnp.where(eidx == i1, -1.0, probs)
    m2 = jnp.max(rest, axis=0, keepdims=True)
    i2 = jnp.min(jnp.where(rest == m2, eidx, n_exp), axis=0, keepdims=True)
    den = m1 + m2
    return i1, i2, m1 / den, m2 / den


def _crossout_body(*refs, alpha, n_exp, npt):
    if n_exp:
        op_ref, os_ref, x1_ref, wmo_ref, g_ref, b_ref, wrh_ref, wrl_ref, x2_ref, ti_ref, tg_ref = refs
    else:
        op_ref, os_ref, x1_ref, wmo_ref, g_ref, b_ref, x2_ref = refs
    o = jnp.where(pl.program_id(0) < npt, op_ref[...], os_ref[...])
    x2 = _ln(alpha * x1_ref[...] + _dot(o, wmo_ref[...]), g_ref[...], b_ref[...])
    x2_ref[...] = x2
    if n_exp:
        xh = x2.astype(BF16)
        xl = (x2 - xh.astype(F32)).astype(BF16)
        wh = wrh_ref[...]
        lt = _dot_nt(wh, xh) + (_dot_nt(wh, xl) + _dot_nt(wrl_ref[...], xh))
        i1, i2, g1, g2 = _top2(lt[:n_exp], n_exp)
        ti_ref[0:1, :] = i1
        ti_ref[1:2, :] = i2
        tg_ref[0:1, :] = g1
        tg_ref[1:2, :] = g2


def _crossout(o_p, o_s, x1, wmo, g, b, l, *, alpha, tm, router=None):
    n, d = x1.shape
    hk = o_p.shape[1]
    assert tm % LANE == 0
    npt = o_p.shape[0] // tm
    in_specs = [*_split_rows(tm, hk, npt), _rows(tm, d), _layer(l, (hk, d)), _layer(l, (1, d)),
                _layer(l, (1, d))]
    out_specs = [_rows(tm, d)]
    out_shape = [jax.ShapeDtypeStruct((n, d), F32)]
    args = [o_p, o_s, x1, wmo, g, b]
    n_exp = 0
    if router is not None:
        wrh, wrl, li, n_exp = router
        er = wrh.shape[1]
        in_specs += [_layer(li, (er, d)), _layer(li, (er, d))]
        top = pl.BlockSpec((2, tm), lambda i: (0, i))
        out_specs += [top, top]
        out_shape += [jax.ShapeDtypeStruct((2, n), jnp.int32), jax.ShapeDtypeStruct((2, n), F32)]
        args += [wrh, wrl]
    body = functools.partial(_crossout_body, alpha=alpha, n_exp=n_exp, npt=npt)
    return pl.pallas_call(
        body, grid=(n // tm,), in_specs=in_specs, out_specs=out_specs, out_shape=out_shape,
        compiler_params=_params("parallel"), name="cross_out")(*args)


def _ffn_body(te_ref, tv_ref, *refs, scaled):
    if scaled:
        x_ref, wg_ref, wu_ref, wd_ref, sc_ref, o_ref, xb_ref = refs
    else:
        x_ref, wg_ref, wu_ref, wd_ref, o_ref, xb_ref = refs
    i = pl.program_id(0)
    j = pl.program_id(1)
    last = pl.num_programs(1) - 1

    @pl.when(tv_ref[i] > 0)
    def _():
        @pl.when(j == 0)
        def _():
            xb_ref[...] = x_ref[...].astype(BF16)
            o_ref[...] = jnp.zeros(o_ref.shape, F32)

        x = xb_ref[...]
        h = (jax.nn.silu(_dot(x, wg_ref[...])) * _dot(x, wu_ref[...])).astype(BF16)
        o_ref[...] += _dot(h, wd_ref[...])

        if scaled:
            @pl.when(j == last)
            def _():
                o_ref[...] *= sc_ref[...]

    @pl.when(tv_ref[i] == 0)
    def _():
        o_ref[...] = jnp.zeros(o_ref.shape, F32)


def _ffn(xs, wg, wu, wd, tile_expert, tile_valid, tm, tf, scale=None):
    rows, d = xs.shape
    ff = wg.shape[2]
    in_specs = [pl.BlockSpec((tm, d), lambda i, j, te, tv: (i, 0)),
                pl.BlockSpec((None, d, tf), lambda i, j, te, tv: (te[i], 0, j)),
                pl.BlockSpec((None, d, tf), lambda i, j, te, tv: (te[i], 0, j)),
                pl.BlockSpec((None, tf, d), lambda i, j, te, tv: (te[i], j, 0))]
    args = [xs, wg, wu, wd]
    if scale is not None:
        in_specs.append(pl.BlockSpec((tm, 1), lambda i, j, te, tv: (i, 0)))
        args.append(scale)
    grid_spec = pltpu.PrefetchScalarGridSpec(
        num_scalar_prefetch=2, grid=(rows // tm, ff // tf), in_specs=in_specs,
        out_specs=pl.BlockSpec((tm, d), lambda i, j, te, tv: (i, 0)),
        scratch_shapes=[pltpu.VMEM((tm, d), BF16)])
    return pl.pallas_call(
        functools.partial(_ffn_body, scaled=scale is not None), grid_spec=grid_spec,
        out_shape=jax.ShapeDtypeStruct((rows, d), F32),
        compiler_params=_params("parallel", "arbitrary"), name="ffn")(tile_expert, tile_valid, *args)


def _addln_body(*refs, alpha, n_y, npt):
    x_ref = refs[0]
    y_refs = refs[1:1 + n_y]
    g_ref, b_ref, op_ref, os_ref = refs[1 + n_y:]
    y = y_refs[0][...]
    for r in y_refs[1:]:
        y = y + r[...]
    out = _ln(alpha * x_ref[...] + y, g_ref[...], b_ref[...])

    @pl.when(pl.program_id(0) < npt)
    def _():
        op_ref[...] = out

    @pl.when(pl.program_id(0) >= npt)
    def _():
        os_ref[...] = out


def _addln(x, ys, g, b, l, *, alpha, tm, n_p):
    n, d = x.shape
    npt = n_p // tm
    body = functools.partial(_addln_body, alpha=alpha, n_y=len(ys), npt=npt)
    return pl.pallas_call(
        body, grid=(n // tm,),
        in_specs=[_rows(tm, d)] * (1 + len(ys)) + [_layer(l, (1, d)), _layer(l, (1, d))],
        out_specs=list(_split_rows(tm, d, npt)),
        out_shape=[jax.ShapeDtypeStruct((n_p, d), F32), jax.ShapeDtypeStruct((n - n_p, d), F32)],
        compiler_params=_params("arbitrary"), name="add_ln")(x, *ys, g, b)


def _cast_cols_body(w_ref, o_ref, *, f):
    o_ref[:, :f] = w_ref[...].astype(BF16)
    if o_ref.shape[1] > f:
        o_ref[:, f:] = jnp.zeros((o_ref.shape[0], o_ref.shape[1] - f), BF16)


def _cast_pad_cols(w, fp):
    e, d, f = w.shape
    td = _tile(d, 512)
    return pl.pallas_call(
        functools.partial(_cast_cols_body, f=f), grid=(e, d // td),
        in_specs=[pl.BlockSpec((None, td, f), lambda a, b: (a, b, 0))],
        out_specs=pl.BlockSpec((None, td, fp), lambda a, b: (a, b, 0)),
        out_shape=jax.ShapeDtypeStruct((e, d, fp), BF16),
        compiler_params=_params("parallel", "parallel"), name="cast_cols")(w)


def _cast_rows_body(w_ref, o_ref, *, f):
    o_ref[:f, :] = w_ref[...].astype(BF16)
    if o_ref.shape[0] > f:
        o_ref[f:, :] = jnp.zeros((o_ref.shape[0] - f, o_ref.shape[1]), BF16)


def _cast_pad_rows(w, fp):
    e, f, d = w.shape
    assert f % (2 * SUBLANE) == 0
    dc = _tile(d, 512, mult=LANE)
    return pl.pallas_call(
        functools.partial(_cast_rows_body, f=f), grid=(e, d // dc),
        in_specs=[pl.BlockSpec((None, f, dc), lambda a, b: (a, 0, b))],
        out_specs=pl.BlockSpec((None, fp, dc), lambda a, b: (a, 0, b)),
        out_shape=jax.ShapeDtypeStruct((e, fp, d), BF16),
        compiler_params=_params("parallel", "parallel"), name="cast_rows")(w)


def _moe_plan(top_i, top_g, n_exp, tm):
    n = top_i.shape[1]
    flat_e = top_i.reshape(-1)
    onehot = (flat_e[:, None] == jnp.arange(n_exp, dtype=jnp.int32)[None, :]).astype(jnp.int32)
    csum = jnp.cumsum(onehot, axis=0)
    rank = jnp.sum(onehot * csum, axis=1) - 1
    counts = csum[-1]
    padded = ((counts + tm - 1) // tm) * tm
    ends = jnp.cumsum(padded)
    starts = ends - padded
    cstart = jnp.cumsum(counts) - counts
    pos = jnp.sum(onehot * starts[None, :], axis=1) + rank
    n_tiles = -(-2 * n // tm) + n_exp
    n_slots = n_tiles * tm
    order = jnp.argsort(flat_e, stable=True).astype(jnp.int32)
    tile_start = jnp.arange(n_tiles, dtype=jnp.int32) * tm
    tile_e = jnp.sum((tile_start[:, None] >= ends[None, :]).astype(jnp.int32), axis=1)
    tile_valid = (tile_e < n_exp).astype(jnp.int32)
    tile_e = jnp.minimum(tile_e, n_exp - 1)
    slot = jnp.arange(n_slots, dtype=jnp.int32)
    slot_e = jnp.repeat(tile_e, tm)
    eh = (slot_e[:, None] == jnp.arange(n_exp, dtype=jnp.int32)[None, :]).astype(jnp.int32)
    within = slot - jnp.sum(eh * starts[None, :], axis=1)
    live = (within < jnp.sum(eh * counts[None, :], axis=1)) & (jnp.repeat(tile_valid, tm) > 0)
    src = jnp.where(live, within + jnp.sum(eh * cstart[None, :], axis=1), slot % (2 * n))
    src_flat = jnp.take(order, src)
    src_token = src_flat % n
    slot_gate = jnp.where(live, jnp.take(top_g.reshape(-1), src_flat), 0.0)
    return src_token, slot_gate[:, None], tile_e, tile_valid, pos[:n], pos[n:]


def kernel(x_prompt, x_sample, mem_prompt, cache_ckv, cache_krope, cache_mem_k, cache_mem_v, state_ssm_re, state_ssm_im, page_table, w_in, g_q, w_uq, g_kv, w_uk, w_uv, ssm_lambda_re, ssm_lambda_im, ssm_log_dt, ssm_b_re, ssm_b_im, ssm_c_re, ssm_c_im, ssm_d, w_glu, g_ssm_out, g_mla_out, w_out, ln1_g, ln1_b, w_mq, w_mk, w_mv, w_mo, ln2_g, ln2_b, w_ff_gate, w_ff_up, w_ff_down, w_router, w_e_gate, w_e_up, w_e_down, ln3_g, ln3_b):
    bp_, tp, d = x_prompt.shape
    bs, ts, _ = x_sample.shape
    depth = w_in.shape[0]
    ssm_w = ssm_d.shape[1]
    groups, pstate = ssm_lambda_re.shape[1:]
    gch = ssm_b_re.shape[-1]
    ql = g_q.shape[1]
    kvl = g_kv.shape[1]
    heads = w_uq.shape[2]
    nope = w_uk.shape[3]
    rope = w_uq.shape[3] - nope
    vdim = w_uv.shape[3]
    mlen = mem_prompt.shape[1]
    mh, md = w_mq.shape[2:]
    n_exp = w_router.shape[2]
    n_pages, page = page_table.shape[1], cache_ckv.shape[2]
    past = n_pages * page
    n_p, n_s = bp_ * tp, bs * ts
    n = n_p + n_s
    assert 2 * rope == LANE and groups % SLAB_GROUPS == 0
    assert SLAB_GROUPS * gch == LANE and bp_ <= SUBLANE // 2
    alpha = (2.0 * depth) ** 0.25
    att_scale = (nope + rope) ** -0.5
    mem_scale = md ** -0.5
    qw = kvl + LANE
    half = rope // 2

    o3 = ssm_w + ql + kvl
    w_in_ext = jnp.concatenate([w_in, w_in[:, :, o3 + half:o3 + rope], w_in[:, :, o3:o3 + half]],
                               axis=2).astype(BF16)
    uq_n, uq_r = w_uq[..., :nope], w_uq[..., nope:]
    w_uq_ext = jnp.concatenate([uq_n, uq_r, uq_r[..., half:], uq_r[..., :half]], axis=-1)
    w_uq_ext = w_uq_ext.reshape(depth, ql, heads * (nope + LANE)).astype(BF16)
    w_ukt = jnp.transpose(w_uk, (0, 2, 3, 1)).astype(BF16)
    w_uv_h = jnp.transpose(w_uv, (0, 2, 1, 3)).astype(BF16)
    w_glu_b = w_glu.astype(BF16)
    w_out_b = w_out.astype(BF16)
    w_mq_b = w_mq.reshape(depth, d, mh * md).astype(BF16)
    w_mk_b = w_mk.reshape(depth, d, mh * md).astype(BF16)
    w_mv_b = w_mv.reshape(depth, d, mh * md).astype(BF16)
    w_mo_b = w_mo.reshape(depth, mh * md, d).astype(BF16)
    row = lambda a: a[:, None, :]
    g_q_r, g_kv_r, d_r = row(g_q), row(g_kv), row(ssm_d)
    g_ssm_r, g_mla_r = row(g_ssm_out), row(g_mla_out)
    ln1g, ln1b, ln2g, ln2b, ln3g, ln3b = (row(a) for a in (ln1_g, ln1_b, ln2_g, ln2_b, ln3_g, ln3_b))

    tf_dense = _tile(w_ff_gate.shape[2], 512, mult=LANE)
    dff = w_ff_gate.shape[2]
    ffg, ffu, ffd = _cast_pad_cols(w_ff_gate, dff), _cast_pad_cols(w_ff_up, dff), _cast_pad_rows(w_ff_down, dff)
    eff = w_e_gate.shape[3]
    tf_moe = min(1024, -(-eff // LANE) * LANE)
    effp = -(-eff // tf_moe) * tf_moe
    n_moe = w_e_gate.shape[0]
    eg = _cast_pad_cols(w_e_gate.reshape(n_moe * n_exp, d, eff), effp)
    eu = _cast_pad_cols(w_e_up.reshape(n_moe * n_exp, d, eff), effp)
    ed = _cast_pad_rows(w_e_down.reshape(n_moe * n_exp, eff, d), effp)
    er = -(-n_exp // 16) * 16
    wr_t = jnp.pad(jnp.transpose(w_router, (0, 2, 1)), ((0, 0), (0, er - n_exp), (0, 0)))
    wr_hi = wr_t.astype(BF16)
    wr_lo = (wr_t - wr_hi.astype(F32)).astype(BF16)

    a_re, a_im, bb_re, bb_im = _s5_prep(ssm_lambda_re[:, :, None, :], ssm_lambda_im[:, :, None, :],
                                        ssm_log_dt[:, :, None, None],
                                        jnp.swapaxes(ssm_b_re, 2, 3), jnp.swapaxes(ssm_b_im, 2, 3))
    nslab = groups // SLAB_GROUPS
    eye = jnp.eye(SLAB_GROUPS, dtype=F32)

    def b_blocks(bb):
        bb = bb.reshape(depth, nslab, SLAB_GROUPS, gch, pstate)
        return jnp.einsum('lsgcp,gh->lsgchp', bb, eye).reshape(
            depth, nslab, SLAB_GROUPS * gch, SLAB_GROUPS * pstate).astype(BF16)

    def c_blocks(c):
        c = c.reshape(depth, nslab, SLAB_GROUPS, gch, pstate)
        return jnp.einsum('lsgcp,gh->lshpgc', c, eye).reshape(
            depth, nslab, SLAB_GROUPS * pstate, SLAB_GROUPS * gch).astype(BF16)

    bre_blk, bim_blk = b_blocks(bb_re), b_blocks(bb_im)
    cre_blk, cim_blk = c_blocks(ssm_c_re), c_blocks(ssm_c_im)
    a_re = a_re.reshape(depth, 1, groups * pstate)
    a_im = a_im.reshape(depth, 1, groups * pstate)
    s5_blk = (jnp.concatenate([bre_blk, bim_blk], axis=2), jnp.concatenate([cre_blk, -cim_blk], axis=3),
              a_re, a_im)
    nbp = SUBLANE // 2
    a1_rows = jnp.broadcast_to(a_re, (depth, 2 * nbp, groups * pstate))
    a2_rows = jnp.concatenate([jnp.broadcast_to(-a_im, (depth, nbp, groups * pstate)),
                               jnp.broadcast_to(a_im, (depth, nbp, groups * pstate))], axis=1)
    s5_tok_blk = (bre_blk, bim_blk, cre_blk, cim_blk, a1_rows, a2_rows)

    pos = jnp.concatenate([jnp.tile(jnp.arange(tp, dtype=F32), bp_),
                           jnp.tile(past + jnp.arange(ts, dtype=F32), bs)])[:, None]
    cs = _rope_table(pos, rope)
    memk, memv = _memkv(mem_prompt.reshape(bp_ * mlen, d), w_mk_b, w_mv_b)
    cache_k = cache_mem_k.reshape(depth, bs, mlen * mh, md)
    cache_v = cache_mem_v.reshape(depth, bs, mlen * mh, md)
    cache_krope_t = jnp.swapaxes(cache_krope, 2, 3)
    kn_rows = LANE
    tm_tok = _tile(math.gcd(tp, n_s), 512)

    def scan_rows(u_bt):
        u_t = jnp.swapaxes(u_bt, 0, 1).astype(BF16)
        zz = jnp.zeros_like(u_t)
        rows = jnp.stack([jnp.concatenate([u_t, zz], -1), jnp.concatenate([zz, u_t], -1)], axis=1)
        return rows.reshape(-1, 2 * ssm_w)

    tm_ffn = _tile(n, 512)
    dense_te = lambda i: jnp.full((n // tm_ffn,), i, jnp.int32)
    dense_tv = jnp.ones((n // tm_ffn,), jnp.int32)

    outs = {k: [] for k in ("p_ckv", "p_kr", "p_sr", "p_si", "s_ckv", "s_kr", "s_sr", "s_si")}
    for l in range(depth):
        if l == 0:
            x, u, cq, kcat, ckv, kr, ckv_t = _inproj(
                (x_prompt.reshape(n_p, d), x_sample.reshape(n_s, d)), w_in_ext, l, g_q_r, g_kv_r, cs,
                ssm_w=ssm_w, ql=ql, kvl=kvl, rope=rope, tm=max(LANE, tm_tok // 2))
        else:
            x, u, cq, kcat, ckv, kr, ckv_t = _inproj(x2, w_in_ext, l, g_q_r, g_kv_r, cs, ssm_w=ssm_w, ql=ql,
                                                     kvl=kvl, rope=rope, tm=max(LANE, tm_tok // 2),
                                                     pre=(ffn_out, ln3g, ln3b, alpha))
        qcat = _qproj(cq, w_uq_ext, w_ukt, l, cs, heads=heads, nope=nope, kvl=kvl,
                      scale=att_scale * math.log2(math.e))

        o_p = _flash(qcat, kcat, ckv_t, batch=bp_, seq=tp, heads=heads, kvl=kvl)
        q_s = qcat[n_p:].reshape(bs, ts, heads, qw).transpose(0, 2, 1, 3).reshape(bs, heads * ts, qw)
        k_new = jnp.pad(kcat[n_p:].reshape(bs, ts, qw), ((0, 0), (0, kn_rows - ts), (0, 0)))
        o_s = _paged(page_table, q_s, k_new, cache_ckv, cache_krope_t, l, ts=ts, kvl=kvl, rope=rope)
        o_s = o_s.reshape(bs, heads, ts, kvl).transpose(0, 2, 1, 3).reshape(n_s, heads * kvl)

        y_p, f_p = _s5_scan_tok(u, s5_tok_blk, l, batch=bp_, seq=tp, nb=nbp)
        s0_s = jnp.concatenate([state_ssm_re[l].reshape(bs, -1), state_ssm_im[l].reshape(bs, -1)], axis=0)
        y2_s, f_s = _s5_scan(scan_rows(u[n_p:].reshape(bs, ts, ssm_w)), bs, s5_blk, l, s0_s)
        y2_s = y2_s.reshape(ts, 2, bs, ssm_w).transpose(1, 2, 0, 3).reshape(2, n_s, ssm_w)
        ssm_n = _s5_post(y_p.reshape(n_p, ssm_w), y2_s, u, d_r, w_glu_b, g_ssm_r, l, tm=tm_tok)

        x1, qm = _mixout(o_p, o_s, ssm_n, x, w_uv_h, g_mla_r, w_out_b, ln1g, ln1b, w_mq_b, l,
                         heads=heads, kvl=kvl, alpha=alpha, mscale=mem_scale, tm=tm_tok)

        c_p = _cross_prompt(qm, memk, memv, l, batch=bp_, seq=tp, mh=mh, md=md)
        c_s = _cross_sample(qm[n_p:], cache_k, cache_v, l, bs=bs, ts=ts, mh=mh, md=md)

        if l % 2 == 0:
            x2, = _crossout(c_p, c_s, x1, w_mo_b, ln2g, ln2b, l, alpha=alpha, tm=tm_tok)
            ffn_out = [_ffn(x2, ffg, ffu, ffd, dense_te(l // 2), dense_tv, tm_ffn, tf_dense)]
        else:
            x2, top_i, top_g = _crossout(c_p, c_s, x1, w_mo_b, ln2g, ln2b, l, alpha=alpha, tm=tm_tok,
                                         router=(wr_hi, wr_lo, l // 2, n_exp))
            tm_moe = 512
            src, gate, tile_e, tile_v, pos1, pos2 = _moe_plan(top_i, top_g, n_exp, tm_moe)
            take = lambda a, idx: a.at[idx].get(mode="promise_in_bounds")
            ys = _ffn(take(x2, src), eg, eu, ed, tile_e + (l // 2) * n_exp, tile_v, tm_moe, tf_moe,
                      scale=gate)
            ffn_out = [take(ys, pos1), take(ys, pos2)]

        outs["p_ckv"].append(ckv[:n_p].reshape(bp_, tp, kvl))
        outs["p_kr"].append(kr[:n_p].reshape(bp_, tp, rope))
        outs["p_sr"].append(f_p[:bp_].reshape(bp_, groups, pstate))
        outs["p_si"].append(f_p[nbp:nbp + bp_].reshape(bp_, groups, pstate))
        outs["s_ckv"].append(ckv[n_p:].reshape(bs, ts, kvl))
        outs["s_kr"].append(kr[n_p:].reshape(bs, ts, rope))
        outs["s_sr"].append(f_s[:bs].reshape(bs, groups, pstate))
        outs["s_si"].append(f_s[bs:].reshape(bs, groups, pstate))

    y_p, y_s = _addln(x2, ffn_out, ln3g, ln3b, depth - 1, alpha=alpha, tm=tm_tok, n_p=n_p)
    st = {k: jnp.stack(v) for k, v in outs.items()}
    return (y_p.reshape(bp_, tp, d), y_s.reshape(bs, ts, d),
            st["p_ckv"], st["p_kr"], st["p_sr"], st["p_si"],
            memk.reshape(depth, bp_, mlen, mh, md), memv.reshape(depth, bp_, mlen, mh, md),
            st["s_ckv"], st["s_kr"], st["s_sr"], st["s_si"])
```

```python
import functools
import math

import jax
import jax.numpy as jnp
from jax import lax
from jax.experimental import pallas as pl
from jax.experimental.pallas import tpu as pltpu

F32 = jnp.float32
BF16 = jnp.bfloat16

LANE = 128
SUBLANE = 8
VMEM_LIMIT = 56 * 2 ** 20
LN_EPS = 1e-5
RMS_EPS = 1e-6
ROPE_BASE = 10000.0
SLAB_GROUPS = 8
NEG = float(jnp.finfo(jnp.float32).min)


def _dot(a, b):
    return jnp.dot(a, b, preferred_element_type=F32)


def _dot_nt(a, b):
    return lax.dot_general(a, b, (((1,), (1,)), ((), ())), preferred_element_type=F32)


def _ln(v, g, b):
    vc = v - jnp.mean(v, -1, keepdims=True)
    var = jnp.mean(vc * vc, -1, keepdims=True)
    return vc * lax.rsqrt(var + LN_EPS) * g + b


def _rms(v, g):
    return v * lax.rsqrt(jnp.mean(v * v, -1, keepdims=True) + RMS_EPS) * g


def _tile(n, pref, mult=SUBLANE):
    best = None
    for t in range(mult, min(n, pref) + 1, mult):
        if n % t == 0:
            best = t
    assert best is not None, (n, pref, mult)
    return best


def _params(*sem):
    return pltpu.CompilerParams(dimension_semantics=sem, vmem_limit_bytes=VMEM_LIMIT)


def _rows(tm, w):
    return pl.BlockSpec((tm, w), lambda i: (i, 0))


def _const(shape):
    return pl.BlockSpec(shape, lambda i: (0,) * len(shape))


def _layer(l, shape):
    return pl.BlockSpec((None,) + shape, lambda i: (l,) + (0,) * len(shape))


def _rope_table_body(pos_ref, inv_ref, cs_ref):
    ang = pos_ref[...] * inv_ref[...]
    lane = lax.broadcasted_iota(jnp.int32, ang.shape, 1)
    c = jnp.cos(ang)
    s = jnp.sin(ang)
    cs_ref[...] = jnp.where(lane < LANE // 2, c, jnp.where(lane < 3 * LANE // 4, -s, s))


def _rope_table(pos, rope):
    n = pos.shape[0]
    half = rope // 2
    inv = ROPE_BASE ** (-jnp.arange(half, dtype=F32) / half)
    inv4 = jnp.tile(inv, 4)[None, :]
    tm = _tile(n, 1024)
    return pl.pallas_call(
        _rope_table_body, grid=(n // tm,),
        in_specs=[_rows(tm, 1), _const((1, LANE))], out_specs=_rows(tm, LANE),
        out_shape=jax.ShapeDtypeStruct((n, LANE), F32), compiler_params=_params("parallel"),
        name="rope_table")(pos, inv4)


def _inproj_body(*refs, ssm_w, ql, kvl, rope, n_y, alpha, npt):
    u_ref, cq_ref, kcat_ref, ckv_ref, kr_ref, ckvt_ref = refs[-6:]
    if npt:
        xp_ref, xs_ref, w_ref, gq_ref, gkv_ref, cs_ref, xo_ref = refs[:-6]
        x = jnp.where(pl.program_id(0) < npt, xp_ref[...], xs_ref[...])
        xo_ref[...] = x
    elif n_y:
        x_ref = refs[0]
        g_ref, b_ref, w_ref, gq_ref, gkv_ref, cs_ref, xo_ref = refs[1 + n_y:-6]
        y = refs[1][...]
        for r in refs[2:1 + n_y]:
            y = y + r[...]
        x = _ln(alpha * x_ref[...] + y, g_ref[...], b_ref[...])
        xo_ref[...] = x
    else:
        x_ref, w_ref, gq_ref, gkv_ref, cs_ref = refs[:-6]
        x = x_ref[...]
    xb = x.astype(BF16)
    o1, o2, o3 = ssm_w, ssm_w + ql, ssm_w + ql + kvl
    u_ref[...] = _dot(xb, w_ref[:, :o1])
    cq_ref[...] = _rms(_dot(xb, w_ref[:, o1:o2]), gq_ref[...]).astype(BF16)
    ckv = _rms(_dot(xb, w_ref[:, o2:o3]), gkv_ref[...])
    ckv_ref[...] = ckv
    ckvt_ref[...] = jnp.transpose(ckv).astype(BF16)
    kr2 = _dot(xb, w_ref[:, o3:o3 + LANE]) * cs_ref[...]
    kr = kr2 + pltpu.roll(kr2, LANE // 2, axis=1)
    kr_ref[...] = kr[:, :rope]
    kcat_ref[:, :kvl] = ckv.astype(BF16)
    kcat_ref[:, kvl:] = kr.astype(BF16)


def _inproj(x, w_ext, l, gq, gkv, cs, *, ssm_w, ql, kvl, rope, tm, pre=None):
    split = isinstance(x, tuple)
    assert not (split and pre is not None)
    n = x[0].shape[0] + x[1].shape[0] if split else x.shape[0]
    d = w_ext.shape[1]
    cols = w_ext.shape[2]
    assert tm % LANE == 0
    ys, alpha = ([], 0.0) if pre is None else (list(pre[0]), pre[3])
    npt = x[0].shape[0] // tm if split else 0
    body = functools.partial(_inproj_body, ssm_w=ssm_w, ql=ql, kvl=kvl, rope=rope, n_y=len(ys), alpha=alpha,
                             npt=npt)
    if split:
        in_specs = list(_split_rows(tm, d, npt))
        args = list(x)
    else:
        in_specs = [_rows(tm, d)] * (1 + len(ys))
        args = [x, *ys]
    out_specs, out_shape = [], []
    if split:
        out_specs.append(_rows(tm, d))
        out_shape.append(jax.ShapeDtypeStruct((n, d), F32))
    if pre is not None:
        in_specs += [_layer(l - 1, (1, d)), _layer(l - 1, (1, d))]
        args += [pre[1], pre[2]]
        out_specs.append(_rows(tm, d))
        out_shape.append(jax.ShapeDtypeStruct((n, d), F32))
    in_specs += [_layer(l, (d, cols)), _layer(l, (1, ql)), _layer(l, (1, kvl)), _rows(tm, LANE)]
    args += [w_ext, gq, gkv, cs]
    out_specs += [_rows(tm, ssm_w), _rows(tm, ql), _rows(tm, kvl + LANE), _rows(tm, kvl),
                  _rows(tm, rope), pl.BlockSpec((kvl, tm), lambda i: (0, i))]
    out_shape += [jax.ShapeDtypeStruct((n, ssm_w), F32), jax.ShapeDtypeStruct((n, ql), BF16),
                  jax.ShapeDtypeStruct((n, kvl + LANE), BF16), jax.ShapeDtypeStruct((n, kvl), F32),
                  jax.ShapeDtypeStruct((n, rope), F32), jax.ShapeDtypeStruct((kvl, n), BF16)]
    return pl.pallas_call(
        body, grid=(n // tm,), in_specs=in_specs, out_specs=out_specs, out_shape=out_shape,
        compiler_params=_params("parallel"), name="in_proj")(*args)


def _qproj_body(cq_ref, wuq_ref, wukt_ref, cs_ref, q_ref, *, heads, nope, kvl, scale):
    q = _dot(cq_ref[...], wuq_ref[...])
    cs = cs_ref[...]
    lane = lax.broadcasted_iota(jnp.int32, cs.shape, 1)
    hw = nope + LANE
    qw = kvl + LANE
    for h in range(heads):
        qlat = _dot(q[:, h * hw:h * hw + nope].astype(BF16), wukt_ref[h])
        q2 = q[:, h * hw + nope:(h + 1) * hw] * cs
        qr = q2 + pltpu.roll(q2, LANE // 2, axis=1)
        qr = jnp.where(lane < LANE // 2, qr, 0.0)
        q_ref[:, h * qw:h * qw + kvl] = (qlat * scale).astype(BF16)
        q_ref[:, h * qw + kvl:(h + 1) * qw] = (qr * scale).astype(BF16)


def _qproj(cq, wuq_ext, wukt, l, cs, *, heads, nope, kvl, scale):
    n, ql = cq.shape
    hw = nope + LANE
    qw = kvl + LANE
    tm = _tile(n, 512)
    body = functools.partial(_qproj_body, heads=heads, nope=nope, kvl=kvl, scale=scale)
    return pl.pallas_call(
        body, grid=(n // tm,),
        in_specs=[_rows(tm, ql), _layer(l, (ql, heads * hw)), _layer(l, (heads, nope, kvl)),
                  _rows(tm, LANE)],
        out_specs=_rows(tm, heads * qw),
        out_shape=jax.ShapeDtypeStruct((n, heads * qw), BF16),
        compiler_params=_params("parallel"), name="q_proj")(cq, wuq_ext, wukt, cs)


def _flash_body(q_ref, k_ref, vt_ref, o_ref, m_ref, l_ref, acc_ref, *, heads, kvl, qw):
    i = pl.program_id(1)
    j = pl.program_id(2)

    @pl.when(j == 0)
    def _():
        m_ref[...] = jnp.full(m_ref.shape, NEG, F32)
        l_ref[...] = jnp.zeros(l_ref.shape, F32)
        acc_ref[...] = jnp.zeros(acc_ref.shape, F32)

    def step(masked):
        k = k_ref[...]
        vt = vt_ref[...]
        for h in range(heads):
            s = _dot_nt(k, q_ref[:, h * qw:(h + 1) * qw])
            if masked:
                key = lax.broadcasted_iota(jnp.int32, s.shape, 0)
                qry = lax.broadcasted_iota(jnp.int32, s.shape, 1)
                s = jnp.where(key <= qry, s, NEG)
            m_prev = m_ref[h]
            m_new = jnp.maximum(m_prev, jnp.max(s, axis=0, keepdims=True))
            a = jnp.exp2(m_prev - m_new)
            p = jnp.exp2(s - m_new)
            l_ref[h] = a * l_ref[h] + jnp.sum(p, axis=0, keepdims=True)
            acc_ref[h] = a * acc_ref[h] + _dot(vt, p.astype(BF16))
            m_ref[h] = m_new

    @pl.when(j < i)
    def _():
        step(False)

    @pl.when(j == i)
    def _():
        step(True)
        for h in range(heads):
            o_ref[:, h * kvl:(h + 1) * kvl] = jnp.transpose(acc_ref[h] / l_ref[h]).astype(BF16)


def _flash(qcat, kcat, ckv_t, *, batch, seq, heads, kvl):
    qw = kvl + LANE
    tq = _tile(seq, 512, mult=LANE)
    nq = seq // tq
    body = functools.partial(_flash_body, heads=heads, kvl=kvl, qw=qw)
    return pl.pallas_call(
        body, grid=(batch, nq, nq),
        in_specs=[pl.BlockSpec((tq, heads * qw), lambda b, i, j: (b * nq + i, 0)),
                  pl.BlockSpec((tq, qw), lambda b, i, j: (b * nq + jnp.minimum(i, j), 0)),
                  pl.BlockSpec((kvl, tq), lambda b, i, j: (0, b * nq + jnp.minimum(i, j)))],
        out_specs=pl.BlockSpec((tq, heads * kvl), lambda b, i, j: (b * nq + i, 0)),
        out_shape=jax.ShapeDtypeStruct((batch * seq, heads * kvl), BF16),
        scratch_shapes=[pltpu.VMEM((heads, 1, tq), F32), pltpu.VMEM((heads, 1, tq), F32),
                        pltpu.VMEM((heads, kvl, tq), F32)],
        compiler_params=_params("parallel", "parallel", "arbitrary"), name="prompt_attn")(
            qcat, kcat, ckv_t)


def _paged_body(pt_ref, q_ref, kn_ref, ckv_hbm, krt_hbm, o_ref, ckv_buf, krt_buf, sem, s_ref, kb_ref,
                *, layer, nseq, n_pages, page, kvl, rope, ts, chunk_pages):
    b = pl.program_id(0)
    nb = pl.num_programs(0)
    slot = lax.rem(b, 2)
    step_pages = nseq * n_pages

    def fetch(bb, sl):
        def issue(i, c):
            for k in range(2):
                p = 2 * i + k
                pg = pt_ref[bb * step_pages + p]
                pltpu.make_async_copy(ckv_hbm.at[layer, pg], ckv_buf.at[sl, p], sem.at[sl, 0]).start(k)
                pltpu.make_async_copy(krt_hbm.at[layer, pg], krt_buf.at[sl, p], sem.at[sl, 1]).start(1 - k)
            return c
        lax.fori_loop(0, step_pages // 2, issue, 0, unroll=4)

    @pl.when(b == 0)
    def _():
        fetch(0, 0)

    @pl.when(b + 1 < nb)
    def _():
        fetch(b + 1, 1 - slot)

    pltpu.make_async_copy(ckv_buf.at[slot], ckv_buf.at[slot], sem.at[slot, 0]).wait()
    pltpu.make_async_copy(krt_buf.at[slot], krt_buf.at[slot], sem.at[slot, 1]).wait()

    ck = chunk_pages * page
    n_chunks = n_pages // chunk_pages
    for sq in range(nseq):
        q = q_ref[sq]
        ql = q[:, :kvl]
        qr = q[:, kvl:kvl + rope]
        for c in range(n_chunks):
            p0 = sq * n_pages + c * chunk_pages
            kc = ckv_buf[slot, pl.ds(p0, chunk_pages)].reshape(ck, kvl).astype(BF16)
            rt = jnp.concatenate([krt_buf[slot, p0 + p] for p in range(chunk_pages)], axis=1).astype(BF16)
            s_ref[sq, :, c * ck:(c + 1) * ck] = _dot_nt(ql, kc) + _dot(qr, rt)
            kb_ref[sq, c * ck:(c + 1) * ck, :] = kc

        kn = kn_ref[sq]
        sn = _dot_nt(q, kn)
        qj = lax.broadcasted_iota(jnp.int32, sn.shape, 0) & (ts - 1)
        col = lax.broadcasted_iota(jnp.int32, sn.shape, 1)
        sn = jnp.where(col <= qj, sn, NEG)

        m = jnp.maximum(jnp.max(s_ref[sq], axis=1, keepdims=True), jnp.max(sn, axis=1, keepdims=True))
        pn = jnp.exp2(sn - m)
        l = jnp.sum(pn, axis=1, keepdims=True)
        acc = _dot(pn.astype(BF16), kn[:, :kvl])
        for c in range(n_chunks):
            p = jnp.exp2(s_ref[sq, :, c * ck:(c + 1) * ck] - m)
            l = l + jnp.sum(p, axis=1, keepdims=True)
            acc = acc + _dot(p.astype(BF16), kb_ref[sq, c * ck:(c + 1) * ck, :])
        o_ref[sq] = (acc / l).astype(BF16)


def _paged(page_table, q_s, k_new, cache_ckv, cache_krope_t, l, *, ts, kvl, rope):
    bs, rows, qw = q_s.shape
    n_pages = page_table.shape[1]
    page = cache_ckv.shape[2]
    assert cache_krope_t.shape[2:] == (rope, page) and n_pages % 2 == 0
    chunk_pages = _tile(n_pages, 16, mult=1)
    past = n_pages * page
    assert ts & (ts - 1) == 0
    nseq = 2 if bs % 2 == 0 else 1
    body = functools.partial(_paged_body, layer=l, nseq=nseq, n_pages=n_pages, page=page, kvl=kvl,
                             rope=rope, ts=ts, chunk_pages=chunk_pages)
    grid_spec = pltpu.PrefetchScalarGridSpec(
        num_scalar_prefetch=1, grid=(bs // nseq,),
        in_specs=[pl.BlockSpec((nseq, rows, qw), lambda b, pt: (b, 0, 0)),
                  pl.BlockSpec((nseq, k_new.shape[1], qw), lambda b, pt: (b, 0, 0)),
                  pl.BlockSpec(memory_space=pl.ANY), pl.BlockSpec(memory_space=pl.ANY)],
        out_specs=pl.BlockSpec((nseq, rows, kvl), lambda b, pt: (b, 0, 0)),
        scratch_shapes=[pltpu.VMEM((2, nseq * n_pages, page, kvl), F32),
                        pltpu.VMEM((2, nseq * n_pages, rope, page), F32),
                        pltpu.SemaphoreType.DMA((2, 2)),
                        pltpu.VMEM((nseq, rows, past), F32),
                        pltpu.VMEM((nseq, past, kvl), BF16)])
    return pl.pallas_call(
        body, grid_spec=grid_spec, out_shape=jax.ShapeDtypeStruct((bs, rows, kvl), BF16),
        compiler_params=_params("arbitrary"), name="paged_attn")(
            page_table.reshape(-1), q_s, k_new, cache_ckv, cache_krope_t)


def _s5_prep_body(lr_ref, li_ref, ldt_ref, br_ref, bi_ref, ar_ref, ai_ref, bbr_ref, bbi_ref):
    lr = lr_ref[...]
    li = li_ref[...]
    dt = jnp.exp(ldt_ref[...])
    mag = jnp.exp(lr * dt)
    ar = mag * jnp.cos(li * dt)
    ai = mag * jnp.sin(li * dt)
    den = lr * lr + li * li
    zr = ((ar - 1.0) * lr + ai * li) / den
    zi = (ai * lr - (ar - 1.0) * li) / den
    ar_ref[...] = ar
    ai_ref[...] = ai
    br = br_ref[...]
    bi = bi_ref[...]
    bbr_ref[...] = zr * br - zi * bi
    bbi_ref[...] = zr * bi + zi * br


def _s5_prep(lam_re, lam_im, log_dt, b_re_t, b_im_t):
    nl, g, _, p = lam_re.shape
    c = b_re_t.shape[2]
    gp = pl.BlockSpec((None, g, 1, p), lambda l: (l, 0, 0, 0))
    gcp = pl.BlockSpec((None, g, c, p), lambda l: (l, 0, 0, 0))
    return pl.pallas_call(
        _s5_prep_body, grid=(nl,),
        in_specs=[gp, gp, pl.BlockSpec((None, g, 1, 1), lambda l: (l, 0, 0, 0)), gcp, gcp],
        out_specs=[gp, gp, gcp, gcp],
        out_shape=[jax.ShapeDtypeStruct((nl, g, 1, p), F32)] * 2
        + [jax.ShapeDtypeStruct((nl, g, c, p), F32)] * 2,
        compiler_params=_params("parallel"), name="s5_prep")(lam_re, lam_im, log_dt, b_re_t, b_im_t)


def _swap_halves(z, nb):
    if 2 * nb == SUBLANE:
        return pltpu.roll(z, nb, axis=0)
    return jnp.concatenate([z[nb:], z[:nb]], axis=0)


def _s5_scan_body(u_ref, bw_ref, cw_ref, ar_ref, ai_ref, s0_ref, y_ref, f_ref, x_ref, st_ref,
                  *, nb, tc, nslab, sc, sp, lw):
    step_rows = 2 * nb
    w = nslab * sc

    @pl.when(pl.program_id(0) == 0)
    def _():
        st_ref[...] = s0_ref[...]

    for j in range(nslab):
        lhs = jnp.concatenate([u_ref[:, j * sc:(j + 1) * sc], u_ref[:, w + j * sc:w + (j + 1) * sc]], axis=1)
        x_ref[:, j * sp:(j + 1) * sp] = _dot(lhs, bw_ref[j])

    im_row = lax.broadcasted_iota(jnp.int32, (step_rows, lw), 0) >= nb
    for p0 in range(0, nslab * sp, lw):
        lanes = slice(p0, p0 + lw)
        a1 = jnp.broadcast_to(ar_ref[:, lanes], (step_rows, lw))
        ai = jnp.broadcast_to(ai_ref[:, lanes], (step_rows, lw))
        a2 = jnp.where(im_row, ai, -ai)
        z = st_ref[:, lanes]
        for t in range(tc):
            rows = slice(t * step_rows, (t + 1) * step_rows)
            z = a1 * z + a2 * _swap_halves(z, nb) + x_ref[rows, lanes]
            x_ref[rows, lanes] = z
        st_ref[:, lanes] = z

    re_row = (lax.broadcasted_iota(jnp.int32, (tc * step_rows, sc), 0) & nb) == 0
    for j in range(nslab):
        y2 = _dot(x_ref[:, j * sp:(j + 1) * sp].astype(BF16), cw_ref[j])
        y_ref[:, j * sc:(j + 1) * sc] = jnp.where(re_row, y2[:, :sc], y2[:, sc:])
    f_ref[...] = st_ref[...]


def _s5_scan(u2, nb, blk, l, s0):
    bw, cw, ar, ai = blk
    rows, w2 = u2.shape
    nslab, sc2, sp = bw.shape[1:]
    sc = sc2 // 2
    w = w2 // 2
    gp = nslab * sp
    step_rows = 2 * nb
    assert nb & (nb - 1) == 0 and step_rows % SUBLANE == 0
    steps = rows // step_rows
    tc = _tile(steps, max(1, 256 // step_rows), mult=1)
    rc = tc * step_rows
    lw = _tile(gp, max(LANE, (8 * SUBLANE * LANE) // step_rows), mult=LANE)
    body = functools.partial(_s5_scan_body, nb=nb, tc=tc, nslab=nslab, sc=sc, sp=sp, lw=lw)
    return pl.pallas_call(
        body, grid=(steps // tc,),
        in_specs=[_rows(rc, w2), _layer(l, (nslab, sc2, sp)), _layer(l, (nslab, sp, sc2)),
                  _layer(l, (1, gp)), _layer(l, (1, gp)), _const((step_rows, gp))],
        out_specs=[_rows(rc, w), _const((step_rows, gp))],
        out_shape=[jax.ShapeDtypeStruct((rows, w), F32), jax.ShapeDtypeStruct((step_rows, gp), F32)],
        scratch_shapes=[pltpu.VMEM((rc, gp), F32), pltpu.VMEM((step_rows, gp), F32)],
        compiler_params=_params("arbitrary"), name="s5_scan")(u2, bw, cw, ar, ai, s0)


def _s5_scan_tok_body(*refs, nreal, nb, tc, pitch, nslab, sc, sp, lw):
    u_refs = refs[:nreal]
    (bre_ref, bim_ref, cre_ref, cim_ref, a1_ref, a2_ref, y_ref, f_ref, ub_ref, x_ref, st_ref) = refs[nreal:]
    step_rows = 2 * nb
    half = nb * pitch
    nq = sp // LANE

    @pl.when(pl.program_id(0) == 0)
    def _():
        st_ref[...] = jnp.zeros(st_ref.shape, F32)
        ub_ref[...] = jnp.zeros(ub_ref.shape, F32)

    for b in range(nreal):
        ub_ref[b * pitch:b * pitch + tc, :] = u_refs[b][...]
    for j in range(nslab):
        uj = ub_ref[:, j * sc:(j + 1) * sc].astype(BF16)
        xr = _dot(uj, bre_ref[j])
        xi = _dot(uj, bim_ref[j])
        for q in range(nq):
            x_ref[j * nq + q, 0:half, :] = xr[:, q * LANE:(q + 1) * LANE]
            x_ref[j * nq + q, half:2 * half, :] = xi[:, q * LANE:(q + 1) * LANE]

    npass = nslab * sp // lw
    nl = lw // LANE
    zs = [st_ref[:, p * lw:(p + 1) * lw] for p in range(npass)]
    for t in range(tc):
        rows = pl.ds(t, step_rows, stride=pitch)
        for p in range(npass):
            lanes = slice(p * lw, (p + 1) * lw)
            x_t = jnp.concatenate([x_ref[p * nl + q, rows, :] for q in range(nl)], axis=1)
            z = a1_ref[:, lanes] * zs[p] + a2_ref[:, lanes] * _swap_halves(zs[p], nb) + x_t
            for q in range(nl):
                x_ref[p * nl + q, rows, :] = z[:, q * LANE:(q + 1) * LANE]
            zs[p] = z
    for p in range(npass):
        st_ref[:, p * lw:(p + 1) * lw] = zs[p]

    for j in range(nslab):
        sr = jnp.concatenate([x_ref[j * nq + q, 0:half, :] for q in range(nq)], axis=1).astype(BF16)
        si = jnp.concatenate([x_ref[j * nq + q, half:2 * half, :] for q in range(nq)], axis=1).astype(BF16)
        y = _dot(sr, cre_ref[j]) - _dot(si, cim_ref[j])
        for b in range(nreal):
            y_ref[b, :, j * sc:(j + 1) * sc] = y[b * pitch:b * pitch + tc]
    f_ref[...] = st_ref[...]


def _s5_scan_tok(u, blk, l, *, batch, seq, nb):
    bre, bim, cre, cim, a1, a2 = blk
    w = u.shape[1]
    nslab, sc, sp = bre.shape[1:]
    gp = nslab * sp
    step_rows = 2 * nb
    assert step_rows == SUBLANE and batch <= nb
    tc = _tile(seq, 128, mult=2 * SUBLANE)
    pitch = tc + SUBLANE // 2
    nt = seq // tc
    lw = _tile(gp, 8 * LANE, mult=LANE)
    body = functools.partial(_s5_scan_tok_body, nreal=batch, nb=nb, tc=tc, pitch=pitch, nslab=nslab,
                             sc=sc, sp=sp, lw=lw)
    u_specs = [pl.BlockSpec((tc, w), functools.partial(lambda i, b: (b * nt + i, 0), b=b))
               for b in range(batch)]
    return pl.pallas_call(
        body, grid=(nt,),
        in_specs=u_specs + [_layer(l, (nslab, sc, sp)), _layer(l, (nslab, sc, sp)),
                            _layer(l, (nslab, sp, sc)), _layer(l, (nslab, sp, sc)),
                            _layer(l, (step_rows, gp)), _layer(l, (step_rows, gp))],
        out_specs=[pl.BlockSpec((batch, tc, w), lambda i: (0, i, 0)), _const((step_rows, gp))],
        out_shape=[jax.ShapeDtypeStruct((batch, seq, w), F32), jax.ShapeDtypeStruct((step_rows, gp), F32)],
        scratch_shapes=[pltpu.VMEM((nb * pitch, w), F32), pltpu.VMEM((gp // LANE, step_rows * pitch, LANE), F32),
                        pltpu.VMEM((step_rows, gp), F32)],
        compiler_params=_params("arbitrary"), name="s5_scan_tok")(
            *([u] * batch), bre, bim, cre, cim, a1, a2)


def _s5_post_body(yp_ref, sre_ref, sim_ref, u_ref, d_ref, wglu_ref, g_ref, o_ref, *, npt):
    y = jnp.where(pl.program_id(0) < npt, yp_ref[...], sre_ref[...] + sim_ref[...])
    yf = y + d_ref[...] * u_ref[...]
    g = jax.nn.gelu(yf)
    z = g * jax.nn.sigmoid(_dot(g.astype(BF16), wglu_ref[...]))
    o_ref[...] = _rms(z, g_ref[...]).astype(BF16)


def _s5_post(y_p, y_s, u, d, wglu, g, l, *, tm):
    n, w = u.shape
    npt = y_p.shape[0] // tm

    def smap(part):
        return pl.BlockSpec((None, tm, w), lambda i: (part, jnp.maximum(i - npt, 0), 0))

    return pl.pallas_call(
        functools.partial(_s5_post_body, npt=npt), grid=(n // tm,),
        in_specs=[_split_rows(tm, w, npt)[0], smap(0), smap(1), _rows(tm, w), _layer(l, (1, w)),
                  _layer(l, (w, w)), _layer(l, (1, w))],
        out_specs=_rows(tm, w), out_shape=jax.ShapeDtypeStruct((n, w), BF16),
        compiler_params=_params("parallel"), name="s5_post")(y_p, y_s, y_s, u, d, wglu, g)


def _split_rows(tm, w, npt):
    return (pl.BlockSpec((tm, w), lambda i: (jnp.minimum(i, npt - 1), 0)),
            pl.BlockSpec((tm, w), lambda i: (jnp.maximum(i - npt, 0), 0)))


def _mixout_body(op_ref, os_ref, ssm_ref, x_ref, wuv_ref, gm_ref, wout_ref, g1_ref, b1_ref, wmq_ref,
                 x1_ref, qm_ref, *, heads, kvl, ssm_w, alpha, mscale, npt):
    is_prompt = pl.program_id(0) < npt

    def o_head(h):
        cols = slice(h * kvl, (h + 1) * kvl)
        return jnp.where(is_prompt, op_ref[:, cols], os_ref[:, cols])

    mla = jnp.concatenate([_dot(o_head(h), wuv_ref[h]) for h in range(heads)], axis=-1)
    mla_n = _rms(mla, gm_ref[...]).astype(BF16)
    m = _dot(ssm_ref[...], wout_ref[:ssm_w, :]) + _dot(mla_n, wout_ref[ssm_w:, :])
    x1 = _ln(alpha * x_ref[...] + m, g1_ref[...], b1_ref[...])
    x1_ref[...] = x1
    qm_ref[...] = (_dot(x1.astype(BF16), wmq_ref[...]) * mscale).astype(BF16)


def _mixout(o_p, o_s, ssm_n, x, wuv, gm, wout, g1, b1, wmq, l, *, heads, kvl, alpha, mscale, tm):
    n, d = x.shape
    ssm_w = ssm_n.shape[1]
    v = wuv.shape[3]
    mix = wout.shape[1]
    mq = wmq.shape[2]
    npt = o_p.shape[0] // tm
    body = functools.partial(_mixout_body, heads=heads, kvl=kvl, ssm_w=ssm_w, alpha=alpha, mscale=mscale,
                             npt=npt)
    return pl.pallas_call(
        body, grid=(n // tm,),
        in_specs=[*_split_rows(tm, heads * kvl, npt), _rows(tm, ssm_w), _rows(tm, d),
                  _layer(l, (heads, kvl, v)), _layer(l, (1, heads * v)), _layer(l, (mix, d)),
                  _layer(l, (1, d)), _layer(l, (1, d)), _layer(l, (d, mq))],
        out_specs=[_rows(tm, d), _rows(tm, mq)],
        out_shape=[jax.ShapeDtypeStruct((n, d), F32), jax.ShapeDtypeStruct((n, mq), BF16)],
        compiler_params=_params("parallel"), name="mix_out")(
            o_p, o_s, ssm_n, x, wuv, gm, wout, g1, b1, wmq)


def _memkv_body(mem_ref, wk_ref, wv_ref, k_ref, v_ref):
    mb = mem_ref[...].astype(BF16)
    k_ref[...] = _dot(mb, wk_ref[...])
    v_ref[...] = _dot(mb, wv_ref[...])


def _memkv(mem, wk, wv):
    nl, d, hk = wk.shape
    rows = mem.shape[0]
    w = pl.BlockSpec((None, d, hk), lambda l: (l, 0, 0))
    o = pl.BlockSpec((None, rows, hk), lambda l: (l, 0, 0))
    return pl.pallas_call(
        _memkv_body, grid=(nl,), in_specs=[_const((rows, d)), w, w], out_specs=[o, o],
        out_shape=[jax.ShapeDtypeStruct((nl, rows, hk), F32)] * 2,
        compiler_params=_params("parallel"), name="mem_kv")(mem, wk, wv)


def _softmax_rows(s):
    p = jnp.exp(s - jnp.max(s, axis=1, keepdims=True))
    return p / jnp.sum(p, axis=1, keepdims=True)


def _cross_p_body(q_ref, mk_ref, mv_ref, o_ref, *, mh, md):
    mk = mk_ref[...].astype(BF16)
    mv = mv_ref[...].astype(BF16)
    for h in range(mh):
        cols = slice(h * md, (h + 1) * md)
        p = _softmax_rows(_dot_nt(q_ref[:, cols], mk[:, cols]))
        o_ref[:, cols] = _dot(p.astype(BF16), mv[:, cols]).astype(BF16)


def _cross_prompt(qm, memk, memv, l, *, batch, seq, mh, md):
    mlen = memk.shape[1] // batch
    hk = mh * md
    tm = _tile(seq, 512)
    nt = seq // tm
    kv = pl.BlockSpec((None, mlen, hk), lambda b, i: (l, b, 0))
    body = functools.partial(_cross_p_body, mh=mh, md=md)
    return pl.pallas_call(
        body, grid=(batch, nt),
        in_specs=[pl.BlockSpec((tm, hk), lambda b, i: (b * nt + i, 0)), kv, kv],
        out_specs=pl.BlockSpec((tm, hk), lambda b, i: (b * nt + i, 0)),
        out_shape=jax.ShapeDtypeStruct((batch * seq, hk), BF16),
        compiler_params=_params("parallel", "parallel"), name="cross_prompt")(qm, memk, memv)


def _cross_s_body(q_ref, mk_ref, mv_ref, o_ref, *, bb, ts, mh, md):
    q_all = q_ref[...].astype(F32)
    rows_kv = mk_ref.shape[1]
    row_head = lax.broadcasted_iota(jnp.int32, (mh * ts, rows_kv), 0) >> (ts.bit_length() - 1)
    col_head = lax.broadcasted_iota(jnp.int32, (mh * ts, rows_kv), 1) & (mh - 1)
    own = row_head == col_head
    outs = []
    for b in range(bb):
        q = q_all[b * ts:(b + 1) * ts, :]
        qh = jnp.concatenate([q[:, h * md:(h + 1) * md] for h in range(mh)], axis=0).astype(BF16)
        s = jnp.where(own, _dot_nt(qh, mk_ref[b].astype(BF16)), NEG)
        o = _dot(_softmax_rows(s).astype(BF16), mv_ref[b].astype(BF16))
        outs.append(jnp.concatenate([o[h * ts:(h + 1) * ts] for h in range(mh)], axis=1))
    o_ref[...] = jnp.concatenate(outs, axis=0).astype(BF16)


def _cross_sample(qm_s, cache_k, cache_v, l, *, bs, ts, mh, md):
    rows_kv = cache_k.shape[2]
    hk = mh * md
    assert mh & (mh - 1) == 0 and ts & (ts - 1) == 0
    bb = _tile(bs, 8, mult=1)
    kv = pl.BlockSpec((None, bb, rows_kv, md), lambda i: (l, i, 0, 0))
    body = functools.partial(_cross_s_body, bb=bb, ts=ts, mh=mh, md=md)
    return pl.pallas_call(
        body, grid=(bs // bb,),
        in_specs=[_rows(bb * ts, hk), kv, kv], out_specs=_rows(bb * ts, hk),
        out_shape=jax.ShapeDtypeStruct((bs * ts, hk), BF16),
        compiler_params=_params("parallel"), name="cross_sample")(qm_s, cache_k, cache_v)


def _top2(lt, n_exp):
    e = jnp.exp(lt - jnp.max(lt, axis=0, keepdims=True))
    probs = e / jnp.sum(e, axis=0, keepdims=True)
    eidx = lax.broadcasted_iota(jnp.int32, probs.shape, 0)
    m1 = jnp.max(probs, axis=0, keepdims=True)
    i1 = jnp.min(jnp.where(probs == m1, eidx, n_exp), axis=0, keepdims=True)
    rest = jnp.where(eidx == i1, -1.0, probs)
    m2 = jnp.max(rest, axis=0, keepdims=True)
    i2 = jnp.min(jnp.where(rest == m2, eidx, n_exp), axis=0, keepdims=True)
    den = m1 + m2
    return i1, i2, m1 / den, m2 / den


def _crossout_body(*refs, alpha, n_exp, npt):
    if n_exp:
        op_ref, os_ref, x1_ref, wmo_ref, g_ref, b_ref, wrh_ref, wrl_ref, x2_ref, ti_ref, tg_ref = refs
    else:
        op_ref, os_ref, x1_ref, wmo_ref, g_ref, b_ref, x2_ref = refs
    o = jnp.where(pl.program_id(0) < npt, op_ref[...], os_ref[...])
    x2 = _ln(alpha * x1_ref[...] + _dot(o, wmo_ref[...]), g_ref[...], b_ref[...])
    x2_ref[...] = x2
    if n_exp:
        xh = x2.astype(BF16)
        xl = (x2 - xh.astype(F32)).astype(BF16)
        wh = wrh_ref[...]
        lt = _dot_nt(wh, xh) + (_dot_nt(wh, xl) + _dot_nt(wrl_ref[...], xh))
        i1, i2, g1, g2 = _top2(lt[:n_exp], n_exp)
        ti_ref[0:1, :] = i1
        ti_ref[1:2, :] = i2
        tg_ref[0:1, :] = g1
        tg_ref[1:2, :] = g2


def _crossout(o_p, o_s, x1, wmo, g, b, l, *, alpha, tm, router=None):
    n, d = x1.shape
    hk = o_p.shape[1]
    assert tm % LANE == 0
    npt = o_p.shape[0] // tm
    in_specs = [*_split_rows(tm, hk, npt), _rows(tm, d), _layer(l, (hk, d)), _layer(l, (1, d)),
                _layer(l, (1, d))]
    out_specs = [_rows(tm, d)]
    out_shape = [jax.ShapeDtypeStruct((n, d), F32)]
    args = [o_p, o_s, x1, wmo, g, b]
    n_exp = 0
    if router is not None:
        wrh, wrl, li, n_exp = router
        er = wrh.shape[1]
        in_specs += [_layer(li, (er, d)), _layer(li, (er, d))]
        top = pl.BlockSpec((2, tm), lambda i: (0, i))
        out_specs += [top, top]
        out_shape += [jax.ShapeDtypeStruct((2, n), jnp.int32), jax.ShapeDtypeStruct((2, n), F32)]
        args += [wrh, wrl]
    body = functools.partial(_crossout_body, alpha=alpha, n_exp=n_exp, npt=npt)
    return pl.pallas_call(
        body, grid=(n // tm,), in_specs=in_specs, out_specs=out_specs, out_shape=out_shape,
        compiler_params=_params("parallel"), name="cross_out")(*args)


def _ffn_body(te_ref, tv_ref, *refs, scaled):
    if scaled:
        x_ref, wg_ref, wu_ref, wd_ref, sc_ref, o_ref, xb_ref = refs
    else:
        x_ref, wg_ref, wu_ref, wd_ref, o_ref, xb_ref = refs
    i = pl.program_id(0)
    j = pl.program_id(1)
    last = pl.num_programs(1) - 1

    @pl.when(tv_ref[i] > 0)
    def _():
        @pl.when(j == 0)
        def _():
            xb_ref[...] = x_ref[...].astype(BF16)
            o_ref[...] = jnp.zeros(o_ref.shape, F32)

        x = xb_ref[...]
        h = (jax.nn.silu(_dot(x, wg_ref[...])) * _dot(x, wu_ref[...])).astype(BF16)
        o_ref[...] += _dot(h, wd_ref[...])

        if scaled:
            @pl.when(j == last)
            def _():
                o_ref[...] *= sc_ref[...]

    @pl.when(tv_ref[i] == 0)
    def _():
        o_ref[...] = jnp.zeros(o_ref.shape, F32)


def _ffn(xs, wg, wu, wd, tile_expert, tile_valid, tm, tf, scale=None):
    rows, d = xs.shape
    ff = wg.shape[2]
    in_specs = [pl.BlockSpec((tm, d), lambda i, j, te, tv: (i, 0)),
                pl.BlockSpec((None, d, tf), lambda i, j, te, tv: (te[i], 0, j)),
                pl.BlockSpec((None, d, tf), lambda i, j, te, tv: (te[i], 0, j)),
                pl.BlockSpec((None, tf, d), lambda i, j, te, tv: (te[i], j, 0))]
    args = [xs, wg, wu, wd]
    if scale is not None:
        in_specs.append(pl.BlockSpec((tm, 1), lambda i, j, te, tv: (i, 0)))
        args.append(scale)
    grid_spec = pltpu.PrefetchScalarGridSpec(
        num_scalar_prefetch=2, grid=(rows // tm, ff // tf), in_specs=in_specs,
        out_specs=pl.BlockSpec((tm, d), lambda i, j, te, tv: (i, 0)),
        scratch_shapes=[pltpu.VMEM((tm, d), BF16)])
    return pl.pallas_call(
        functools.partial(_ffn_body, scaled=scale is not None), grid_spec=grid_spec,
        out_shape=jax.ShapeDtypeStruct((rows, d), F32),
        compiler_params=_params("parallel", "arbitrary"), name="ffn")(tile_expert, tile_valid, *args)


def _addln_body(*refs, alpha, n_y, npt):
    x_ref = refs[0]
    y_refs = refs[1:1 + n_y]
    g_ref, b_ref, op_ref, os_ref = refs[1 + n_y:]
    y = y_refs[0][...]
    for r in y_refs[1:]:
        y = y + r[...]
    out = _ln(alpha * x_ref[...] + y, g_ref[...], b_ref[...])

    @pl.when(pl.program_id(0) < npt)
    def _():
        op_ref[...] = out

    @pl.when(pl.program_id(0) >= npt)
    def _():
        os_ref[...] = out


def _addln(x, ys, g, b, l, *, alpha, tm, n_p):
    n, d = x.shape
    npt = n_p // tm
    body = functools.partial(_addln_body, alpha=alpha, n_y=len(ys), npt=npt)
    return pl.pallas_call(
        body, grid=(n // tm,),
        in_specs=[_rows(tm, d)] * (1 + len(ys)) + [_layer(l, (1, d)), _layer(l, (1, d))],
        out_specs=list(_split_rows(tm, d, npt)),
        out_shape=[jax.ShapeDtypeStruct((n_p, d), F32), jax.ShapeDtypeStruct((n - n_p, d), F32)],
        compiler_params=_params("arbitrary"), name="add_ln")(x, *ys, g, b)


def _cast_cols_body(w_ref, o_ref, *, f):
    o_ref[:, :f] = w_ref[...].astype(BF16)
    if o_ref.shape[1] > f:
        o_ref[:, f:] = jnp.zeros((o_ref.shape[0], o_ref.shape[1] - f), BF16)


def _cast_pad_cols(w, fp):
    e, d, f = w.shape
    td = _tile(d, 512)
    return pl.pallas_call(
        functools.partial(_cast_cols_body, f=f), grid=(e, d // td),
        in_specs=[pl.BlockSpec((None, td, f), lambda a, b: (a, b, 0))],
        out_specs=pl.BlockSpec((None, td, fp), lambda a, b: (a, b, 0)),
        out_shape=jax.ShapeDtypeStruct((e, d, fp), BF16),
        compiler_params=_params("parallel", "parallel"), name="cast_cols")(w)


def _cast_rows_body(w_ref, o_ref, *, f):
    o_ref[:f, :] = w_ref[...].astype(BF16)
    if o_ref.shape[0] > f:
        o_ref[f:, :] = jnp.zeros((o_ref.shape[0] - f, o_ref.shape[1]), BF16)


def _cast_pad_rows(w, fp):
    e, f, d = w.shape
    assert f % (2 * SUBLANE) == 0
    dc = _tile(d, 512, mult=LANE)
    return pl.pallas_call(
        functools.partial(_cast_rows_body, f=f), grid=(e, d // dc),
        in_specs=[pl.BlockSpec((None, f, dc), lambda a, b: (a, 0, b))],
        out_specs=pl.BlockSpec((None, fp, dc), lambda a, b: (a, 0, b)),
        out_shape=jax.ShapeDtypeStruct((e, fp, d), BF16),
        compiler_params=_params("parallel", "parallel"), name="cast_rows")(w)


def _moe_plan(top_i, top_g, n_exp, tm):
    n = top_i.shape[1]
    flat_e = top_i.reshape(-1)
    onehot = (flat_e[:, None] == jnp.arange(n_exp, dtype=jnp.int32)[None, :]).astype(jnp.int32)
    csum = jnp.cumsum(onehot, axis=0)
    rank = jnp.sum(onehot * csum, axis=1) - 1
    counts = csum[-1]
    padded = ((counts + tm - 1) // tm) * tm
    ends = jnp.cumsum(padded)
    starts = ends - padded
    cstart = jnp.cumsum(counts) - counts
    pos = jnp.sum(onehot * starts[None, :], axis=1) + rank
    n_tiles = -(-2 * n // tm) + n_exp
    n_slots = n_tiles * tm
    order = jnp.argsort(flat_e, stable=True).astype(jnp.int32)
    tile_start = jnp.arange(n_tiles, dtype=jnp.int32) * tm
    tile_e = jnp.sum((tile_start[:, None] >= ends[None, :]).astype(jnp.int32), axis=1)
    tile_valid = (tile_e < n_exp).astype(jnp.int32)
    tile_e = jnp.minimum(tile_e, n_exp - 1)
    slot = jnp.arange(n_slots, dtype=jnp.int32)
    slot_e = jnp.repeat(tile_e, tm)
    eh = (slot_e[:, None] == jnp.arange(n_exp, dtype=jnp.int32)[None, :]).astype(jnp.int32)
    within = slot - jnp.sum(eh * starts[None, :], axis=1)
    live = (within < jnp.sum(eh * counts[None, :], axis=1)) & (jnp.repeat(tile_valid, tm) > 0)
    src = jnp.where(live, within + jnp.sum(eh * cstart[None, :], axis=1), slot % (2 * n))
    src_flat = jnp.take(order, src)
    src_token = src_flat % n
    slot_gate = jnp.where(live, jnp.take(top_g.reshape(-1), src_flat), 0.0)
    return src_token, slot_gate[:, None], tile_e, tile_valid, pos[:n], pos[n:]


def kernel(x_prompt, x_sample, mem_prompt, cache_ckv, cache_krope, cache_mem_k, cache_mem_v, state_ssm_re, state_ssm_im, page_table, w_in, g_q, w_uq, g_kv, w_uk, w_uv, ssm_lambda_re, ssm_lambda_im, ssm_log_dt, ssm_b_re, ssm_b_im, ssm_c_re, ssm_c_im, ssm_d, w_glu, g_ssm_out, g_mla_out, w_out, ln1_g, ln1_b, w_mq, w_mk, w_mv, w_mo, ln2_g, ln2_b, w_ff_gate, w_ff_up, w_ff_down, w_router, w_e_gate, w_e_up, w_e_down, ln3_g, ln3_b):
    bp_, tp, d = x_prompt.shape
    bs, ts, _ = x_sample.shape
    depth = w_in.shape[0]
    ssm_w = ssm_d.shape[1]
    groups, pstate = ssm_lambda_re.shape[1:]
    gch = ssm_b_re.shape[-1]
    ql = g_q.shape[1]
    kvl = g_kv.shape[1]
    heads = w_uq.shape[2]
    nope = w_uk.shape[3]
    rope = w_uq.shape[3] - nope
    vdim = w_uv.shape[3]
    mlen = mem_prompt.shape[1]
    mh, md = w_mq.shape[2:]
    n_exp = w_router.shape[2]
    n_pages, page = page_table.shape[1], cache_ckv.shape[2]
    past = n_pages * page
    n_p, n_s = bp_ * tp, bs * ts
    n = n_p + n_s
    assert 2 * rope == LANE and groups % SLAB_GROUPS == 0
    assert SLAB_GROUPS * gch == LANE and bp_ <= SUBLANE // 2
    alpha = (2.0 * depth) ** 0.25
    att_scale = (nope + rope) ** -0.5
    mem_scale = md ** -0.5
    qw = kvl + LANE
    half = rope // 2

    o3 = ssm_w + ql + kvl
    w_in_ext = jnp.concatenate([w_in, w_in[:, :, o3 + half:o3 + rope], w_in[:, :, o3:o3 + half]],
                               axis=2).astype(BF16)
    uq_n, uq_r = w_uq[..., :nope], w_uq[..., nope:]
    w_uq_ext = jnp.concatenate([uq_n, uq_r, uq_r[..., half:], uq_r[..., :half]], axis=-1)
    w_uq_ext = w_uq_ext.reshape(depth, ql, heads * (nope + LANE)).astype(BF16)
    w_ukt = jnp.transpose(w_uk, (0, 2, 3, 1)).astype(BF16)
    w_uv_h = jnp.transpose(w_uv, (0, 2, 1, 3)).astype(BF16)
    w_glu_b = w_glu.astype(BF16)
    w_out_b = w_out.astype(BF16)
    w_mq_b = w_mq.reshape(depth, d, mh * md).astype(BF16)
    w_mk_b = w_mk.reshape(depth, d, mh * md).astype(BF16)
    w_mv_b = w_mv.reshape(depth, d, mh * md).astype(BF16)
    w_mo_b = w_mo.reshape(depth, mh * md, d).astype(BF16)
    row = lambda a: a[:, None, :]
    g_q_r, g_kv_r, d_r = row(g_q), row(g_kv), row(ssm_d)
    g_ssm_r, g_mla_r = row(g_ssm_out), row(g_mla_out)
    ln1g, ln1b, ln2g, ln2b, ln3g, ln3b = (row(a) for a in (ln1_g, ln1_b, ln2_g, ln2_b, ln3_g, ln3_b))

    tf_dense = _tile(w_ff_gate.shape[2], 512, mult=LANE)
    dff = w_ff_gate.shape[2]
    ffg, ffu, ffd = _cast_pad_cols(w_ff_gate, dff), _cast_pad_cols(w_ff_up, dff), _cast_pad_rows(w_ff_down, dff)
    eff = w_e_gate.shape[3]
    tf_moe = min(1024, -(-eff // LANE) * LANE)
    effp = -(-eff // tf_moe) * tf_moe
    n_moe = w_e_gate.shape[0]
    eg = _cast_pad_cols(w_e_gate.reshape(n_moe * n_exp, d, eff), effp)
    eu = _cast_pad_cols(w_e_up.reshape(n_moe * n_exp, d, eff), effp)
    ed = _cast_pad_rows(w_e_down.reshape(n_moe * n_exp, eff, d), effp)
    er = -(-n_exp // 16) * 16
    wr_t = jnp.pad(jnp.transpose(w_router, (0, 2, 1)), ((0, 0), (0, er - n_exp), (0, 0)))
    wr_hi = wr_t.astype(BF16)
    wr_lo = (wr_t - wr_hi.astype(F32)).astype(BF16)

    a_re, a_im, bb_re, bb_im = _s5_prep(ssm_lambda_re[:, :, None, :], ssm_lambda_im[:, :, None, :],
                                        ssm_log_dt[:, :, None, None],
                                        jnp.swapaxes(ssm_b_re, 2, 3), jnp.swapaxes(ssm_b_im, 2, 3))
    nslab = groups // SLAB_GROUPS
    eye = jnp.eye(SLAB_GROUPS, dtype=F32)

    def b_blocks(bb):
        bb = bb.reshape(depth, nslab, SLAB_GROUPS, gch, pstate)
        return jnp.einsum('lsgcp,gh->lsgchp', bb, eye).reshape(
            depth, nslab, SLAB_GROUPS * gch, SLAB_GROUPS * pstate).astype(BF16)

    def c_blocks(c):
        c = c.reshape(depth, nslab, SLAB_GROUPS, gch, pstate)
        return jnp.einsum('lsgcp,gh->lshpgc', c, eye).reshape(
            depth, nslab, SLAB_GROUPS * pstate, SLAB_GROUPS * gch).astype(BF16)

    bre_blk, bim_blk = b_blocks(bb_re), b_blocks(bb_im)
    cre_blk, cim_blk = c_blocks(ssm_c_re), c_blocks(ssm_c_im)
    a_re = a_re.reshape(depth, 1, groups * pstate)
    a_im = a_im.reshape(depth, 1, groups * pstate)
    s5_blk = (jnp.concatenate([bre_blk, bim_blk], axis=2), jnp.concatenate([cre_blk, -cim_blk], axis=3),
              a_re, a_im)
    nbp = SUBLANE // 2
    a1_rows = jnp.broadcast_to(a_re, (depth, 2 * nbp, groups * pstate))
    a2_rows = jnp.concatenate([jnp.broadcast_to(-a_im, (depth, nbp, groups * pstate)),
                               jnp.broadcast_to(a_im, (depth, nbp, groups * pstate))], axis=1)
    s5_tok_blk = (bre_blk, bim_blk, cre_blk, cim_blk, a1_rows, a2_rows)

    pos = jnp.concatenate([jnp.tile(jnp.arange(tp, dtype=F32), bp_),
                           jnp.tile(past + jnp.arange(ts, dtype=F32), bs)])[:, None]
    cs = _rope_table(pos, rope)
    memk, memv = _memkv(mem_prompt.reshape(bp_ * mlen, d), w_mk_b, w_mv_b)
    cache_k = cache_mem_k.reshape(depth, bs, mlen * mh, md)
    cache_v = cache_mem_v.reshape(depth, bs, mlen * mh, md)
    cache_krope_t = jnp.swapaxes(cache_krope, 2, 3)
    kn_rows = LANE
    tm_tok = _tile(math.gcd(tp, n_s), 512)

    def scan_rows(u_bt):
        u_t = jnp.swapaxes(u_bt, 0, 1).astype(BF16)
        zz = jnp.zeros_like(u_t)
        rows = jnp.stack([jnp.concatenate([u_t, zz], -1), jnp.concatenate([zz, u_t], -1)], axis=1)
        return rows.reshape(-1, 2 * ssm_w)

    tm_ffn = _tile(n, 512)
    dense_te = lambda i: jnp.full((n // tm_ffn,), i, jnp.int32)
    dense_tv = jnp.ones((n // tm_ffn,), jnp.int32)

    outs = {k: [] for k in ("p_ckv", "p_kr", "p_sr", "p_si", "s_ckv", "s_kr", "s_sr", "s_si")}
    for l in range(depth):
        if l == 0:
            x, u, cq, kcat, ckv, kr, ckv_t = _inproj(
                (x_prompt.reshape(n_p, d), x_sample.reshape(n_s, d)), w_in_ext, l, g_q_r, g_kv_r, cs,
                ssm_w=ssm_w, ql=ql, kvl=kvl, rope=rope, tm=max(LANE, tm_tok // 2))
        else:
            x, u, cq, kcat, ckv, kr, ckv_t = _inproj(x2, w_in_ext, l, g_q_r, g_kv_r, cs, ssm_w=ssm_w, ql=ql,
                                                     kvl=kvl, rope=rope, tm=max(LANE, tm_tok // 2),
                                                     pre=(ffn_out, ln3g, ln3b, alpha))
        qcat = _qproj(cq, w_uq_ext, w_ukt, l, cs, heads=heads, nope=nope, kvl=kvl,
                      scale=att_scale * math.log2(math.e))

        o_p = _flash(qcat, kcat, ckv_t, batch=bp_, seq=tp, heads=heads, kvl=kvl)
        q_s = qcat[n_p:].reshape(bs, ts, heads, qw).transpose(0, 2, 1, 3).reshape(bs, heads * ts, qw)
        k_new = jnp.pad(kcat[n_p:].reshape(bs, ts, qw), ((0, 0), (0, kn_rows - ts), (0, 0)))
        o_s = _paged(page_table, q_s, k_new, cache_ckv, cache_krope_t, l, ts=ts, kvl=kvl, rope=rope)
        o_s = o_s.reshape(bs, heads, ts, kvl).transpose(0, 2, 1, 3).reshape(n_s, heads * kvl)

        y_p, f_p = _s5_scan_tok(u, s5_tok_blk, l, batch=bp_, seq=tp, nb=nbp)
        s0_s = jnp.concatenate([state_ssm_re[l].reshape(bs, -1), state_ssm_im[l].reshape(bs, -1)], axis=0)
        y2_s, f_s = _s5_scan(scan_rows(u[n_p:].reshape(bs, ts, ssm_w)), bs, s5_blk, l, s0_s)
        y2_s = y2_s.reshape(ts, 2, bs, ssm_w).transpose(1, 2, 0, 3).reshape(2, n_s, ssm_w)
        ssm_n = _s5_post(y_p.reshape(n_p, ssm_w), y2_s, u, d_r, w_glu_b, g_ssm_r, l, tm=tm_tok)

        x1, qm = _mixout(o_p, o_s, ssm_n, x, w_uv_h, g_mla_r, w_out_b, ln1g, ln1b, w_mq_b, l,
                         heads=heads, kvl=kvl, alpha=alpha, mscale=mem_scale, tm=tm_tok)

        c_p = _cross_prompt(qm, memk, memv, l, batch=bp_, seq=tp, mh=mh, md=md)
        c_s = _cross_sample(qm[n_p:], cache_k, cache_v, l, bs=bs, ts=ts, mh=mh, md=md)

        if l % 2 == 0:
            x2, = _crossout(c_p, c_s, x1, w_mo_b, ln2g, ln2b, l, alpha=alpha, tm=tm_tok)
            ffn_out = [_ffn(x2, ffg, ffu, ffd, dense_te(l // 2), dense_tv, tm_ffn, tf_dense)]
        else:
            x2, top_i, top_g = _crossout(c_p, c_s, x1, w_mo_b, ln2g, ln2b, l, alpha=alpha, tm=tm_tok,
                                         router=(wr_hi, wr_lo, l // 2, n_exp))
            tm_moe = 512
            src, gate, tile_e, tile_v, pos1, pos2 = _moe_plan(top_i, top_g, n_exp, tm_moe)
            take = lambda a, idx: a.at[idx].get(mode="promise_in_bounds")
            ys = _ffn(take(x2, src), eg, eu, ed, tile_e + (l // 2) * n_exp, tile_v, tm_moe, tf_moe,
                      scale=gate)
            ffn_out = [take(ys, pos1), take(ys, pos2)]

        outs["p_ckv"].append(ckv[:n_p].reshape(bp_, tp, kvl))
        outs["p_kr"].append(kr[:n_p].reshape(bp_, tp, rope))
        outs["p_sr"].append(f_p[:bp_].reshape(bp_, groups, pstate))
        outs["p_si"].append(f_p[nbp:nbp + bp_].reshape(bp_, groups, pstate))
        outs["s_ckv"].append(ckv[n_p:].reshape(bs, ts, kvl))
        outs["s_kr"].append(kr[n_p:].reshape(bs, ts, rope))
        outs["s_sr"].append(f_s[:bs].reshape(bs, groups, pstate))
        outs["s_si"].append(f_s[bs:].reshape(bs, groups, pstate))

    y_p, y_s = _addln(x2, ffn_out, ln3g, ln3b, depth - 1, alpha=alpha, tm=tm_tok, n_p=n_p)
    st = {k: jnp.stack(v) for k, v in outs.items()}
    return (y_p.reshape(bp_, tp, d), y_s.reshape(bs, ts, d),
            st["p_ckv"], st["p_kr"], st["p_sr"], st["p_si"],
            memk.reshape(depth, bp_, mlen, mh, md), memv.reshape(depth, bp_, mlen, mh, md),
            st["s_ckv"], st["s_kr"], st["s_sr"], st["s_si"])
```
